```python
import functools
import jax, jax.numpy as jnp
from jax import lax
import numpy as np

D_MODEL = 1024
BATCH = 4
SEQ = 8192
DEPTH = 2
DEC_BATCH = 32
DEC_SEQ = 1
PAST_LEN = 16384
PAGE_SIZE = 128

HEAD_DIM = 64
WIDTH_A = D_MODEL // 2
N_HEADS_A = WIDTH_A // HEAD_DIM
WIDTH_B = D_MODEL // 4
POOL_WINDOWS = (2, 4, 8, 16)
POOL_GROUP = WIDTH_B // len(POOL_WINDOWS)
POOL_STATE = max(POOL_WINDOWS) - 1
WIDTH_C = D_MODEL // 4
CONV_K = 3
MIX_WIDTH = WIDTH_A + WIDTH_B + WIDTH_C
PROJ_WIDTH = 3 * WIDTH_A + WIDTH_B + 3 * WIDTH_C
MOBA_BLOCK = 256
MOBA_TOPK = 3
Q_CHUNK = 128
ROPE_THETA = 10000.0
D_FF = ((8 * D_MODEL // 3 + 127) // 128) * 128
RMS_EPS = 1e-6

kernel_name = "hymba_moba_pool_shortconv_decode_step"


def rms_norm(x, g):
    xf = x.astype(jnp.float32)
    y = xf * lax.rsqrt(jnp.mean(xf * xf, axis=-1, keepdims=True) + RMS_EPS)
    return (y * g.astype(jnp.float32)).astype(x.dtype)


def swiglu(x, wg, wu, wd):
    return (jax.nn.silu(x @ wg) * (x @ wu)) @ wd


def rope(x, pos):
    half = HEAD_DIM // 2
    inv = jnp.power(ROPE_THETA, -jnp.arange(half, dtype=jnp.float32) / half)
    ang = pos.astype(jnp.float32)[:, None] * inv[None, :]
    cos = jnp.cos(ang)[None, :, None, :]
    sin = jnp.sin(ang)[None, :, None, :]
    xf = x.astype(jnp.float32)
    x1, x2 = xf[..., :half], xf[..., half:]
    return jnp.concatenate([x1 * cos - x2 * sin, x2 * cos + x1 * sin], axis=-1).astype(x.dtype)


def pool_mixer(u_ext, pos, w_pool, pool_scale):
    B = u_ext.shape[0]
    T = pos.shape[0]
    uf = u_ext.astype(jnp.float32)
    cs = jnp.concatenate([jnp.zeros_like(uf[:, :1]), jnp.cumsum(uf, axis=1)], axis=1)
    end = cs[:, POOL_STATE + 1:]
    u_t = uf[:, POOL_STATE:]
    outs = []
    for g, w in enumerate(POOL_WINDOWS):
        sl = slice(g * POOL_GROUP, (g + 1) * POOL_GROUP)
        start = cs[:, POOL_STATE + 1 - w: POOL_STATE + 1 - w + T, sl]
        cnt = jnp.minimum(pos + 1, w).astype(jnp.float32)[None, :, None]
        outs.append((end[..., sl] - start) / cnt - u_t[..., sl])
    d = jnp.stack(outs, axis=2)
    y = jnp.einsum('btgc,gcd->btgd', d, w_pool.astype(jnp.float32)).reshape(B, T, WIDTH_B)
    return (y * pool_scale.astype(jnp.float32)).astype(u_ext.dtype)


def short_conv(z_ext, conv_w):
    T = z_ext.shape[1] - (CONV_K - 1)
    y = z_ext[:, 0:T] * conv_w[0]
    for j in range(1, CONV_K):
        y = y + z_ext[:, j:j + T] * conv_w[j]
    return y


def moba_attend(q, q_pos, kmean, gather_kv):
    Q = q.shape[2]
    scale = HEAD_DIM ** -0.5
    qf = q.astype(jnp.float32)
    q_blk = q_pos // MOBA_BLOCK
    span = q_blk[0] * MOBA_BLOCK + jnp.arange(MOBA_BLOCK + Q)
    k_o, v_o = gather_kv(span[None, None, :])
    s = jnp.einsum('bhqd,bhrd->bhqr', qf, k_o.astype(jnp.float32)) * scale
    ok = (span[None, :] >= (q_blk * MOBA_BLOCK)[:, None]) & (span[None, :] <= q_pos[:, None])
    s = jnp.where(ok, s, -jnp.inf)
    m = s.max(axis=-1)
    p = jnp.exp(s - m[..., None])
    l = p.sum(axis=-1)
    acc = jnp.einsum('bhqr,bhrd->bhqd', p, v_o.astype(jnp.float32))
    nbf = kmean.shape[1]
    n_sel = min(MOBA_TOPK, nbf)
    if n_sel > 0:
        gate = jnp.einsum('bhqd,bnhd->bhqn', qf, kmean)
        fully_past = jnp.arange(nbf)[None, :] < q_blk[:, None]
        gate = jnp.where(fully_past, gate, -jnp.inf)
        g_val, g_idx = lax.top_k(gate, n_sel)
        for j in range(n_sel):
            rows = g_idx[..., j, None] * MOBA_BLOCK + jnp.arange(MOBA_BLOCK)
            k_s, v_s = gather_kv(rows)
            s = jnp.einsum('bhqd,bhqrd->bhqr', qf, k_s.astype(jnp.float32)) * scale
            s = jnp.where(jnp.isfinite(g_val[..., j])[..., None], s, -jnp.inf)
            m_new = jnp.maximum(m, s.max(axis=-1))
            alpha = jnp.exp(m - m_new)
            p = jnp.exp(s - m_new[..., None])
            l = l * alpha + p.sum(axis=-1)
            acc = acc * alpha[..., None] + jnp.einsum('bhqr,bhqrd->bhqd', p, v_s.astype(jnp.float32))
            m = m_new
    return (acc / l[..., None]).astype(q.dtype)


def _batch_head_index(B, rows):
    shape = (1,) * (rows.ndim - 2)
    b = jnp.arange(B).reshape((B, 1) + shape)
    h = jnp.arange(N_HEADS_A).reshape((1, N_HEADS_A) + shape)
    return b, h


def attend_prompt(q, k, v, pos):
    B, S = q.shape[0], q.shape[1]
    nbf = S // MOBA_BLOCK
    kmean = k[:, :nbf * MOBA_BLOCK].astype(jnp.float32).reshape(
        B, nbf, MOBA_BLOCK, N_HEADS_A, HEAD_DIM).mean(axis=2)

    def gather_kv(rows):
        rows = jnp.clip(rows, 0, S - 1)
        b, h = _batch_head_index(B, rows)
        return k[b, rows, h], v[b, rows, h]

    nc = S // Q_CHUNK
    qc = q.transpose(0, 2, 1, 3).reshape(B, N_HEADS_A, nc, Q_CHUNK, HEAD_DIM).transpose(2, 0, 1, 3, 4)
    pc = pos.reshape(nc, Q_CHUNK)
    out = lax.map(lambda a: moba_attend(a[0], a[1], kmean, gather_kv), (qc, pc))
    return out.transpose(1, 0, 3, 2, 4).reshape(B, S, WIDTH_A)


def attend_sample(q, k_new, v_new, pos, cache_k, cache_v, page_table, layer):
    Bd, T = q.shape[0], q.shape[1]
    L = PAST_LEN + T
    nbf = L // MOBA_BLOCK
    k_past = cache_k[layer, page_table].reshape(Bd, PAST_LEN, N_HEADS_A, HEAD_DIM)
    k_log = jnp.concatenate([k_past, k_new.astype(k_past.dtype)], axis=1)
    kmean = k_log[:, :nbf * MOBA_BLOCK].astype(jnp.float32).reshape(
        Bd, nbf, MOBA_BLOCK, N_HEADS_A, HEAD_DIM).mean(axis=2)

    def gather_kv(rows):
        rows = jnp.clip(rows, 0, L - 1)
        b, h = _batch_head_index(Bd, rows)
        in_past = (rows < PAST_LEN)[..., None]
        pr = jnp.minimum(rows, PAST_LEN - 1)
        phys = page_table[b, pr // PAGE_SIZE]
        off = pr % PAGE_SIZE
        nr = jnp.clip(rows - PAST_LEN, 0, T - 1)
        kk = jnp.where(in_past, cache_k[layer, phys, off, h], k_new[b, nr, h])
        vv = jnp.where(in_past, cache_v[layer, phys, off, h], v_new[b, nr, h])
        return kk, vv

    out = moba_attend(q.transpose(0, 2, 1, 3), pos, kmean, gather_kv)
    return out.transpose(0, 2, 1, 3).reshape(Bd, T, WIDTH_A)


def token_mixing(h, pos, attend, pool_prev, conv_prev, w_in, w_pool, pool_scale, conv_w, w_o):
    B, T, _ = h.shape
    o1 = WIDTH_A
    o2 = 2 * WIDTH_A
    o3 = 3 * WIDTH_A
    o4 = o3 + WIDTH_B
    o5 = o4 + WIDTH_C
    o6 = o5 + WIDTH_C
    proj = h @ w_in
    q, k, v, u, hc, bg, cg = jnp.split(proj, [o1, o2, o3, o4, o5, o6], axis=-1)
    q = rope(q.reshape(B, T, N_HEADS_A, HEAD_DIM), pos)
    k = rope(k.reshape(B, T, N_HEADS_A, HEAD_DIM), pos)
    v = v.reshape(B, T, N_HEADS_A, HEAD_DIM)
    a = attend(q, k, v)
    u_ext = jnp.concatenate([pool_prev.astype(u.dtype), u], axis=1)
    p = pool_mixer(u_ext, pos, w_pool, pool_scale)
    z = cg * hc
    z_ext = jnp.concatenate([conv_prev.astype(z.dtype), z], axis=1)
    c = bg * short_conv(z_ext, conv_w)
    out = jnp.concatenate([a, p, c], axis=-1) @ w_o
    return out, k, v, u_ext[:, -POOL_STATE:], z_ext[:, -(CONV_K - 1):]


def trunk_layer(x, pos, attend, pool_prev, conv_prev, n1, g1, u1, d1, nm, w_in, w_pool,
                pool_scale, conv_w, w_o, n2, g2, u2, d2):
    x = x + 0.5 * swiglu(rms_norm(x, n1), g1, u1, d1)
    m, k, v, pool_tail, conv_tail = token_mixing(rms_norm(x, nm), pos, attend, pool_prev, conv_prev,
                                                 w_in, w_pool, pool_scale, conv_w, w_o)
    x = x + m
    x = x + 0.5 * swiglu(rms_norm(x, n2), g2, u2, d2)
    return x, k, v, pool_tail, conv_tail


def setup_inputs(seed: int = 0) -> dict:
    key = jax.random.key(seed)
    ks = jax.random.split(key, 24)
    n_pages = PAST_LEN // PAGE_SIZE
    n_pool = (DEC_BATCH * n_pages * 5) // 4
    nrm = jax.random.normal
    f32 = jnp.float32
    page_table = jax.random.permutation(ks[4], n_pool)[:DEC_BATCH * n_pages].reshape(
        DEC_BATCH, n_pages).astype(jnp.int32)
    return {
        "x_prompt": nrm(ks[0], (BATCH, SEQ, D_MODEL), f32),
        "x_sample": nrm(ks[1], (DEC_BATCH, DEC_SEQ, D_MODEL), f32),
        "cache_k": nrm(ks[2], (DEPTH, n_pool, PAGE_SIZE, N_HEADS_A, HEAD_DIM), f32),
        "cache_v": nrm(ks[3], (DEPTH, n_pool, PAGE_SIZE, N_HEADS_A, HEAD_DIM), f32),
        "page_table": page_table,
        "state_pool": nrm(ks[5], (DEPTH, DEC_BATCH, POOL_STATE, WIDTH_B), f32),
        "state_conv": nrm(ks[6], (DEPTH, DEC_BATCH, CONV_K - 1, WIDTH_C), f32),
        "norm_ffn1": 1.0 + 0.02 * nrm(ks[7], (DEPTH, D_MODEL), f32),
        "ffn1_gate": nrm(ks[8], (DEPTH, D_MODEL, D_FF), f32) * D_MODEL ** -0.5,
        "ffn1_up": nrm(ks[9], (DEPTH, D_MODEL, D_FF), f32) * D_MODEL ** -0.5,
        "ffn1_down": nrm(ks[10], (DEPTH, D_FF, D_MODEL), f32) * D_FF ** -0.5,
        "norm_mix": 1.0 + 0.02 * nrm(ks[11], (DEPTH, D_MODEL), f32),
        "w_in": nrm(ks[12], (DEPTH, D_MODEL, PROJ_WIDTH), f32) * D_MODEL ** -0.5,
        "w_pool": nrm(ks[13], (DEPTH, len(POOL_WINDOWS), POOL_GROUP, POOL_GROUP), f32) * POOL_GROUP ** -0.5,
        "pool_scale": 1.0 + 0.02 * nrm(ks[14], (DEPTH, WIDTH_B), f32),
        "conv_w": nrm(ks[15], (DEPTH, CONV_K, WIDTH_C), f32) * CONV_K ** -0.5,
        "w_o": nrm(ks[16], (DEPTH, MIX_WIDTH, D_MODEL), f32) * MIX_WIDTH ** -0.5,
        "norm_ffn2": 1.0 + 0.02 * nrm(ks[17], (DEPTH, D_MODEL), f32),
        "ffn2_gate": nrm(ks[18], (DEPTH, D_MODEL, D_FF), f32) * D_MODEL ** -0.5,
        "ffn2_up": nrm(ks[19], (DEPTH, D_MODEL, D_FF), f32) * D_MODEL ** -0.5,
        "ffn2_down": nrm(ks[20], (DEPTH, D_FF, D_MODEL), f32) * D_FF ** -0.5,
        "norm_final": 1.0 + 0.02 * nrm(ks[21], (D_MODEL,), f32),
    }


def reference(x_prompt, x_sample, cache_k, cache_v, page_table, state_pool, state_conv,
              norm_ffn1, ffn1_gate, ffn1_up, ffn1_down, norm_mix, w_in, w_pool, pool_scale,
              conv_w, w_o, norm_ffn2, ffn2_gate, ffn2_up, ffn2_down, norm_final):
    B, S = x_prompt.shape[0], x_prompt.shape[1]
    T = x_sample.shape[1]
    pos_p = jnp.arange(S, dtype=jnp.int32)
    pos_s = PAST_LEN + jnp.arange(T, dtype=jnp.int32)
    zero_pool = jnp.zeros((B, POOL_STATE, WIDTH_B), x_prompt.dtype)
    zero_conv = jnp.zeros((B, CONV_K - 1, WIDTH_C), x_prompt.dtype)
    attend_p = functools.partial(attend_prompt, pos=pos_p)
    xp, xs = x_prompt, x_sample
    kp_l, vp_l, ks_l, vs_l, pp_l, ps_l, cp_l, cs_l = [], [], [], [], [], [], [], []
    for l in range(DEPTH):
        lw = (norm_ffn1[l], ffn1_gate[l], ffn1_up[l], ffn1_down[l], norm_mix[l], w_in[l], w_pool[l],
              pool_scale[l], conv_w[l], w_o[l], norm_ffn2[l], ffn2_gate[l], ffn2_up[l], ffn2_down[l])
        xp, kp, vp, pp, cp = trunk_layer(xp, pos_p, attend_p, zero_pool, zero_conv, *lw)
        attend_s = functools.partial(attend_sample, pos=pos_s, cache_k=cache_k, cache_v=cache_v,
                                     page_table=page_table, layer=l)
        xs, ksm, vsm, psm, csm = trunk_layer(xs, pos_s, attend_s, state_pool[l], state_conv[l], *lw)
        kp_l.append(kp); vp_l.append(vp); pp_l.append(pp); cp_l.append(cp)
        ks_l.append(ksm); vs_l.append(vsm); ps_l.append(psm); cs_l.append(csm)
    y_prompt = rms_norm(xp, norm_final)
    y_sample = rms_norm(xs, norm_final)
    return (y_prompt, y_sample, jnp.stack(kp_l), jnp.stack(vp_l), jnp.stack(ks_l), jnp.stack(vs_l),
            jnp.stack(pp_l), jnp.stack(ps_l), jnp.stack(cp_l), jnp.stack(cs_l))
```

```python
import functools

import jax
import jax.numpy as jnp
from jax import lax
from jax.experimental import pallas as pl
from jax.experimental.pallas import tpu as pltpu

F32 = jnp.float32
BF16 = jnp.bfloat16

HEAD_DIM = 64
LANES = 128
MOBA_BLOCK = 256
MOBA_TOPK = 3
POOL_WINDOWS = (2, 4, 8, 16)
POOL_STATE = max(POOL_WINDOWS) - 1
POOL_CARRY = 16
CONV_K = 3
CONV_CARRY = 8
ROPE_THETA = 10000.0
RMS_EPS = 1e-6
NEG = -1e30
VMEM_LIMIT = 56 * 1024 * 1024


def _dot(a, b):
    return jnp.dot(a, b, preferred_element_type=F32)


def _rms(x, g):
    return x * lax.rsqrt(jnp.mean(x * x, axis=-1, keepdims=True) + RMS_EPS) * g


def _resident(shape):
    return pl.BlockSpec(shape, lambda *_: (0,) * len(shape), pipeline_mode=pl.Buffered(1))


def _params(sem):
    return pltpu.CompilerParams(dimension_semantics=sem, vmem_limit_bytes=VMEM_LIMIT)


def _ffn_kernel(*refs, premix, final_norm, ff_chunks):
    it = iter(refs)
    x_ref = next(it)
    if premix:
        a_ref, pc_ref, wo_ref = next(it), next(it), next(it)
    n_ref, wg_ref, wu_ref, wd_ref = next(it), next(it), next(it), next(it)
    if final_norm:
        nf_ref = next(it)
    o_ref = next(it)

    x = x_ref[...]
    if premix:
        wa = a_ref.shape[-1]
        x = x + _dot(a_ref[...], wo_ref[:wa, :]) + _dot(pc_ref[...], wo_ref[wa:, :])
    h = _rms(x, n_ref[...]).astype(BF16)
    y = jnp.zeros_like(x)
    for lo, hi in ff_chunks:
        g = _dot(h, wg_ref[:, lo:hi])
        u = _dot(h, wu_ref[:, lo:hi])
        act = (g * jax.nn.sigmoid(g) * u).astype(BF16)
        y = y + _dot(act, wd_ref[lo:hi, :])
    x = x + 0.5 * y
    if final_norm:
        x = _rms(x, nf_ref[...])
    o_ref[...] = x


def _ffn(x, norm, wg, wu, wd, *, mix=None, final=None, tm):
    m, d = x.shape
    ff = wg.shape[1]
    half = (ff // 2) // LANES * LANES
    ff_chunks = ((0, half), (half, ff))
    row = lambda w: pl.BlockSpec((tm, w), lambda i: (i, 0))
    args, specs = [x], [row(d)]
    if mix is not None:
        a, pc, wo = mix
        args += [a, pc, wo]
        specs += [row(a.shape[1]), row(pc.shape[1]), _resident(wo.shape)]
    args += [norm, wg, wu, wd]
    specs += [_resident(norm.shape), _resident(wg.shape), _resident(wu.shape), _resident(wd.shape)]
    if final is not None:
        args.append(final)
        specs.append(_resident(final.shape))
    return pl.pallas_call(
        functools.partial(_ffn_kernel, premix=mix is not None, final_norm=final is not None,
                          ff_chunks=ff_chunks),
        grid=(m // tm,),
        in_specs=specs,
        out_specs=row(d),
        out_shape=jax.ShapeDtypeStruct((m, d), F32),
        compiler_params=_params(("parallel",)),
        name="ffn",
    )(*args)


def _rope_chunk(x, cos, sin_signed):
    lane = lax.broadcasted_iota(jnp.int32, x.shape, 1)
    first_half = (lane % HEAD_DIM) < (HEAD_DIM // 2)
    partner = jnp.where(first_half, pltpu.roll(x, LANES - HEAD_DIM // 2, 1),
                        pltpu.roll(x, HEAD_DIM // 2, 1))
    return x * cos + partner * sin_signed


def _pool_select(s2, s4, s8, s16, pos):
    lane = lax.broadcasted_iota(jnp.int32, s2.shape, 1)
    group = s2.shape[1] // len(POOL_WINDOWS)
    posf = (pos + 1).astype(F32)
    mean = lambda s, w: s / jnp.minimum(posf, float(w))
    return jnp.where(lane < group, mean(s2, 2),
                     jnp.where(lane < 2 * group, mean(s4, 4),
                               jnp.where(lane < 3 * group, mean(s8, 8), mean(s16, 16))))


def _rope_tables(pos):
    half = HEAD_DIM // 2
    inv = jnp.power(ROPE_THETA, -jnp.arange(half, dtype=F32) / half)
    ang = pos.astype(F32)[:, None] * inv[None, :]
    cos, sin = jnp.cos(ang), jnp.sin(ang)
    reps = LANES // HEAD_DIM
    return (jnp.tile(jnp.concatenate([cos, cos], axis=1), (1, reps)),
            jnp.tile(jnp.concatenate([-sin, sin], axis=1), (1, reps)))


def _proj_kernel(x_ref, nm_ref, win_ref, cos_ref, sin_ref, wpool_ref, pscale_ref, convw_ref,
                 k_ref, v_ref, q_ref, kt_ref, va_ref, pc_ref, km_ref, ptail_ref, ctail_ref,
                 prevu_ref, prevz_ref, *, wa, wb, wc):
    i = pl.program_id(1)
    tm = x_ref.shape[1]
    n_chunks = wa // LANES

    @pl.when(i == 0)
    def _():
        prevu_ref[...] = jnp.zeros_like(prevu_ref)
        prevz_ref[...] = jnp.zeros_like(prevz_ref)

    h = _rms(x_ref[0], nm_ref[...]).astype(BF16)
    cos, sin = cos_ref[...], sin_ref[...]
    lane = lax.broadcasted_iota(jnp.int32, (tm, LANES), 1)
    low = lane < HEAD_DIM
    block_onehot = (lane - HEAD_DIM == i).astype(F32)
    ones_col = (lane == HEAD_DIM).astype(F32)
    scale = HEAD_DIM ** -0.5

    for c in range(n_chunks):
        sl = slice(c * LANES, (c + 1) * LANES)
        q = _rope_chunk(_dot(h, win_ref[:, sl]), cos, sin) * scale
        k = _rope_chunk(_dot(h, win_ref[:, wa + c * LANES: wa + (c + 1) * LANES]), cos, sin)
        v = _dot(h, win_ref[:, 2 * wa + c * LANES: 2 * wa + (c + 1) * LANES])
        k_ref[0, :, sl] = k
        v_ref[0, :, sl] = v
        km_ref[0, 0, :, sl] = jnp.mean(k, axis=0, keepdims=True)
        for hh in range(2):
            head = 2 * c + hh
            qh, kh, vh = (q, k, v) if hh == 0 else (
                pltpu.roll(q, HEAD_DIM, 1), pltpu.roll(k, HEAD_DIM, 1), pltpu.roll(v, HEAD_DIM, 1))
            q_ref[0, head] = jnp.where(low, qh, 0.0).astype(BF16)
            kt_ref[0, head, 0] = jnp.where(low, kh, block_onehot).T.astype(BF16)
            va_ref[0, head, 0] = jnp.where(low, vh, ones_col).astype(BF16)

    o = 3 * wa
    u = _dot(h, win_ref[:, o:o + wb])
    hc = _dot(h, win_ref[:, o + wb:o + wb + wc])
    bg = _dot(h, win_ref[:, o + wb + wc:o + wb + 2 * wc])
    cg = _dot(h, win_ref[:, o + wb + 2 * wc:o + wb + 3 * wc])

    pos = i * tm + lax.broadcasted_iota(jnp.int32, (tm, 1), 0)
    e = jnp.concatenate([prevu_ref[...], u], axis=0)
    s2 = e + pltpu.roll(e, 1, 0)
    s4 = s2 + pltpu.roll(s2, 2, 0)
    s8 = s4 + pltpu.roll(s4, 4, 0)
    s16 = s8 + pltpu.roll(s8, 8, 0)
    t = slice(POOL_CARRY, POOL_CARRY + tm)
    d = _pool_select(s2[t], s4[t], s8[t], s16[t], pos) - u
    p = _dot(d.astype(BF16), wpool_ref[...]) * pscale_ref[...]
    prevu_ref[...] = u[tm - POOL_CARRY:]
    ptail_ref[0] = u[tm - POOL_CARRY:]

    z = cg * hc
    ze = jnp.concatenate([prevz_ref[...], z], axis=0)
    tz = slice(CONV_CARRY, CONV_CARRY + tm)
    conv = (z * convw_ref[2:3, :] + pltpu.roll(ze, 1, 0)[tz] * convw_ref[1:2, :]
            + pltpu.roll(ze, 2, 0)[tz] * convw_ref[0:1, :])
    prevz_ref[...] = z[tm - CONV_CARRY:]
    ctail_ref[0] = z[tm - CONV_CARRY:]
    pc_ref[0, :, :wb] = p.astype(BF16)
    pc_ref[0, :, wb:] = (bg * conv).astype(BF16)


def _proj_prompt(x, nm, w_in, cos, sin, wpool_bd, pscale, conv_w, *, n_heads):
    b, s, d = x.shape
    tm = MOBA_BLOCK
    nb = s // tm
    wa = n_heads * HEAD_DIM
    wb = wpool_bd.shape[0]
    wc = conv_w.shape[1]
    assert s % tm == 0 and nb <= LANES - HEAD_DIM and wa % LANES == 0
    assert w_in.shape[1] == 3 * wa + wb + 3 * wc
    seq = lambda w: pl.BlockSpec((1, tm, w), lambda bi, i: (bi, i, 0))
    tab = pl.BlockSpec((tm, LANES), lambda bi, i: (i, 0))
    per_b = lambda r, w: pl.BlockSpec((1, r, w), lambda bi, i: (bi, 0, 0))
    out_shape = (
        jax.ShapeDtypeStruct((b, s, wa), F32),
        jax.ShapeDtypeStruct((b, s, wa), F32),
        jax.ShapeDtypeStruct((b, n_heads, s, LANES), BF16),
        jax.ShapeDtypeStruct((b, n_heads, nb, LANES, tm), BF16),
        jax.ShapeDtypeStruct((b, n_heads, nb, tm, LANES), BF16),
        jax.ShapeDtypeStruct((b, s, wb + wc), BF16),
        jax.ShapeDtypeStruct((b, nb, 1, wa), F32),
        jax.ShapeDtypeStruct((b, POOL_CARRY, wb), F32),
        jax.ShapeDtypeStruct((b, CONV_CARRY, wc), F32),
    )
    out_specs = (
        seq(wa), seq(wa),
        pl.BlockSpec((1, n_heads, tm, LANES), lambda bi, i: (bi, 0, i, 0)),
        pl.BlockSpec((1, n_heads, 1, LANES, tm), lambda bi, i: (bi, 0, i, 0, 0)),
        pl.BlockSpec((1, n_heads, 1, tm, LANES), lambda bi, i: (bi, 0, i, 0, 0)),
        seq(wb + wc), pl.BlockSpec((1, 1, 1, wa), lambda bi, i: (bi, i, 0, 0)),
        per_b(POOL_CARRY, wb), per_b(CONV_CARRY, wc),
    )
    return pl.pallas_call(
        functools.partial(_proj_kernel, wa=wa, wb=wb, wc=wc),
        grid=(b, nb),
        in_specs=[seq(d), _resident(nm.shape), _resident(w_in.shape), tab, tab,
                  _resident(wpool_bd.shape), _resident(pscale.shape), _resident(conv_w.shape)],
        out_specs=out_specs,
        out_shape=out_shape,
        scratch_shapes=[pltpu.VMEM((POOL_CARRY, wb), F32), pltpu.VMEM((CONV_CARRY, wc), F32)],
        compiler_params=_params(("parallel", "arbitrary")),
        name="proj_prompt",
    )(x, nm, w_in, cos, sin, wpool_bd, pscale, conv_w)


def _top_blocks(gate, n_valid):
    lane = lax.broadcasted_iota(jnp.int32, gate.shape, 1)
    lane_f = lane.astype(F32)
    valid = (lane >= HEAD_DIM) & (lane - HEAD_DIM < n_valid)
    g = jnp.where(valid, gate, -jnp.inf)
    picked = jnp.zeros(gate.shape, jnp.bool_)
    for _ in range(MOBA_TOPK):
        m = jnp.max(g, axis=1, keepdims=True)
        first = jnp.min(jnp.where(g == m, lane_f, float(2 * LANES)), axis=1, keepdims=True)
        hit = (lane_f == first) & (m > -jnp.inf)
        picked = picked | hit
        g = jnp.where(hit, -jnp.inf, g)
    return picked


def _attn_kernel(q_ref, kt_ref, va_ref, kmt_ref, o_ref):
    i = pl.program_id(2)
    tq = q_ref.shape[2]
    lane = lax.broadcasted_iota(jnp.int32, (tq, LANES), 1)
    row = lax.broadcasted_iota(jnp.int32, (tq, MOBA_BLOCK), 0)
    col = lax.broadcasted_iota(jnp.int32, (tq, MOBA_BLOCK), 1)
    outs = []
    for hh in range(2):
        q = q_ref[0, hh]
        km = kmt_ref[0, hh]
        km_hi = km.astype(BF16)
        km_lo = (km - km_hi.astype(F32)).astype(BF16)
        picked = _top_blocks(_dot(q, km_hi) + _dot(q, km_lo), i)
        q_sel = jnp.where((lane >= HEAD_DIM) & jnp.logical_not(picked), NEG, q.astype(F32)).astype(BF16)

        s = jnp.where(col <= row, _dot(q, kt_ref[0, hh, i]), NEG)
        m = jnp.max(s, axis=1, keepdims=True)
        acc = _dot(jnp.exp(s - m).astype(BF16), va_ref[0, hh, i])

        def body(j, carry):
            m, acc = carry
            s = _dot(q_sel, kt_ref[0, hh, j])
            m_new = jnp.maximum(m, jnp.max(s, axis=1, keepdims=True))
            p = jnp.exp(s - m_new).astype(BF16)
            return m_new, acc * jnp.exp(m - m_new) + _dot(p, va_ref[0, hh, j])

        m, acc = lax.fori_loop(0, i, body, (m, acc))
        outs.append(acc / acc[:, HEAD_DIM:HEAD_DIM + 1])
    o_ref[0] = jnp.where(lane < HEAD_DIM, outs[0], pltpu.roll(outs[1], HEAD_DIM, 1)).astype(BF16)


def _attn_prompt(q, kt, va, kmt):
    b, n_heads, s, _ = q.shape
    nb = kt.shape[2]
    tq = MOBA_BLOCK
    return pl.pallas_call(
        _attn_kernel,
        grid=(b, n_heads // 2, nb),
        in_specs=[
            pl.BlockSpec((1, 2, tq, LANES), lambda bi, hp, i: (bi, hp, i, 0)),
            pl.BlockSpec((1, 2, nb, LANES, tq), lambda bi, hp, i: (bi, hp, 0, 0, 0)),
            pl.BlockSpec((1, 2, nb, tq, LANES), lambda bi, hp, i: (bi, hp, 0, 0, 0)),
            pl.BlockSpec((1, 2, LANES, LANES), lambda bi, hp, i: (bi, hp, 0, 0)),
        ],
        out_specs=pl.BlockSpec((1, tq, LANES), lambda bi, hp, i: (bi, i, hp)),
        out_shape=jax.ShapeDtypeStruct((b, s, n_heads * HEAD_DIM), BF16),
        compiler_params=_params(("parallel", "parallel", "arbitrary")),
        name="attn_prompt",
    )(q, kt, va, kmt)


def _gate_table(kmean, n_heads):
    b, nb = kmean.shape[:2]
    t = kmean.reshape(b, nb, n_heads, HEAD_DIM).transpose(0, 2, 3, 1)
    return jnp.pad(t, ((0, 0), (0, 0), (0, LANES - HEAD_DIM), (HEAD_DIM, LANES - HEAD_DIM - nb)))


def _sproj_kernel(x_ref, nm_ref, win_ref, cos_ref, sin_ref, wpool_ref, pscale_ref, convw_ref,
                  spool_ref, sconv_ref, q_ref, k_ref, v_ref, pc_ref, u_ref, z_ref,
                  *, wa, wb, wc, pos):
    h = _rms(x_ref[...], nm_ref[...]).astype(BF16)
    cos, sin = cos_ref[...], sin_ref[...]
    scale = HEAD_DIM ** -0.5
    for c in range(wa // LANES):
        sl = slice(c * LANES, (c + 1) * LANES)
        q_ref[:, sl] = _rope_chunk(_dot(h, win_ref[:, sl]), cos, sin) * scale
        k_ref[:, sl] = _rope_chunk(_dot(h, win_ref[:, wa + c * LANES: wa + (c + 1) * LANES]), cos, sin)
        v_ref[:, sl] = _dot(h, win_ref[:, 2 * wa + c * LANES: 2 * wa + (c + 1) * LANES])
    o = 3 * wa
    u = _dot(h, win_ref[:, o:o + wb])
    hc = _dot(h, win_ref[:, o + wb:o + wb + wc])
    bg = _dot(h, win_ref[:, o + wb + wc:o + wb + 2 * wc])
    cg = _dot(h, win_ref[:, o + wb + 2 * wc:o + wb + 3 * wc])

    sums, run = {}, u
    for r in range(1, max(POOL_WINDOWS)):
        run = run + spool_ref[:, POOL_STATE - r, :]
        if r + 1 in POOL_WINDOWS:
            sums[r + 1] = run
    posv = jnp.full((u.shape[0], 1), pos, jnp.int32)
    d = _pool_select(sums[2], sums[4], sums[8], sums[16], posv) - u
    p = _dot(d.astype(BF16), wpool_ref[...]) * pscale_ref[...]

    z = cg * hc
    conv = (z * convw_ref[2:3, :] + sconv_ref[:, 1, :] * convw_ref[1:2, :]
            + sconv_ref[:, 0, :] * convw_ref[0:1, :])
    u_ref[...] = u
    z_ref[...] = z
    pc_ref[:, :wb] = p.astype(BF16)
    pc_ref[:, wb:] = (bg * conv).astype(BF16)


def _proj_sample(x, nm, w_in, cos, sin, wpool_bd, pscale, conv_w, spool, sconv, *, n_heads, pos):
    m, _ = x.shape
    wa = n_heads * HEAD_DIM
    wb = wpool_bd.shape[0]
    wc = conv_w.shape[1]
    assert spool.shape[1] == POOL_STATE and sconv.shape[1] == CONV_K - 1
    f = lambda w, dt: jax.ShapeDtypeStruct((m, w), dt)
    return pl.pallas_call(
        functools.partial(_sproj_kernel, wa=wa, wb=wb, wc=wc, pos=pos),
        out_shape=(f(wa, F32), f(wa, F32), f(wa, F32), f(wb + wc, BF16), f(wb, F32), f(wc, F32)),
        compiler_params=pltpu.CompilerParams(vmem_limit_bytes=VMEM_LIMIT),
        name="proj_sample",
    )(x, nm, w_in, cos, sin, wpool_bd, pscale, conv_w, spool, sconv)


PAGES_PER_STEP = 16


def _sgate_kernel(pt_ref, *refs, n_heads, page_size):
    page_refs = refs[:PAGES_PER_STEP]
    q_ref, idx_ref, km_ref = refs[PAGES_PER_STEP:]
    g = pl.program_id(1)
    pages_per_block = MOBA_BLOCK // page_size
    blocks_per_step = PAGES_PER_STEP // pages_per_block
    means = []
    for r in range(blocks_per_step):
        tot = jnp.sum(page_refs[r * pages_per_block][0, 0], axis=0, keepdims=True)
        for t in range(1, pages_per_block):
            tot = tot + jnp.sum(page_refs[r * pages_per_block + t][0, 0], axis=0, keepdims=True)
        means.append(tot / float(MOBA_BLOCK))
    first_block = pl.multiple_of(g * blocks_per_step, blocks_per_step)
    km_ref[pl.ds(first_block, blocks_per_step), :] = jnp.concatenate(means, axis=0)

    @pl.when(g == pl.num_programs(1) - 1)
    def _():
        nbp, width = km_ref.shape
        prod = km_ref[...] * q_ref[0]
        chan = lax.broadcasted_iota(jnp.int32, (width, LANES), 0)
        head = lax.broadcasted_iota(jnp.int32, (width, LANES), 1)
        seg = (chan // HEAD_DIM == head).astype(F32)
        gate = jnp.dot(prod, seg, preferred_element_type=F32, precision=lax.Precision.HIGHEST)
        blk = lax.broadcasted_iota(jnp.int32, gate.shape, 0).astype(F32)
        rows = []
        for _ in range(MOBA_TOPK):
            mx = jnp.max(gate, axis=0, keepdims=True)
            first = jnp.min(jnp.where(gate == mx, blk, float(nbp)), axis=0, keepdims=True)
            gate = jnp.where(blk == first, -jnp.inf, gate)
            rows.append(first)
        rows.append(jnp.zeros((8 - MOBA_TOPK, LANES), F32))
        idx_ref[0] = jnp.concatenate(rows, axis=0).astype(jnp.int32)


def _sample_gates(cache_k_layer, page_table, q3, *, n_heads):
    n_pool, page_size, width = cache_k_layer.shape
    b, n_pages = page_table.shape
    nbp = n_pages * page_size // MOBA_BLOCK
    assert MOBA_BLOCK % page_size == 0 and PAGES_PER_STEP == 8 * (MOBA_BLOCK // page_size)
    assert n_pages % PAGES_PER_STEP == 0 and nbp >= MOBA_TOPK
    page_spec = lambda r: pl.BlockSpec(
        (1, 1, page_size, width), lambda bi, g, pt: (0, pt[bi, g * PAGES_PER_STEP + r], 0, 0))
    ck = cache_k_layer.reshape(1, n_pool, page_size, width)
    return pl.pallas_call(
        functools.partial(_sgate_kernel, n_heads=n_heads, page_size=page_size),
        grid_spec=pltpu.PrefetchScalarGridSpec(
            num_scalar_prefetch=1,
            grid=(b, n_pages // PAGES_PER_STEP),
            in_specs=[page_spec(r) for r in range(PAGES_PER_STEP)]
            + [pl.BlockSpec((1, 1, width), lambda bi, g, pt: (bi, 0, 0))],
            out_specs=pl.BlockSpec((1, 8, LANES), lambda bi, g, pt: (bi, 0, 0)),
            scratch_shapes=[pltpu.VMEM((nbp, width), F32)],
        ),
        out_shape=jax.ShapeDtypeStruct((b, 8, LANES), jnp.int32),
        compiler_params=_params(("parallel", "arbitrary")),
        name="sample_gates",
    )(page_table, *([ck] * PAGES_PER_STEP), q3)


def _sattn_kernel(idx_ref, pt_ref, *refs, n_fetch):
    k_refs = refs[:2 * n_fetch]
    v_refs = refs[2 * n_fetch:4 * n_fetch]
    q_ref, kn_ref, vn_ref, o_ref = refs[4 * n_fetch:]
    q = q_ref[0]
    lane = lax.broadcasted_iota(jnp.int32, (1, LANES), 1)
    out = jnp.zeros((1, LANES), F32)
    for hh in range(2):
        mine = (lane >= hh * HEAD_DIM) & (lane < (hh + 1) * HEAD_DIM)
        qh = jnp.where(mine, q, 0.0)
        s_self = jnp.sum(qh * kn_ref[0], axis=1, keepdims=True)
        scores = [jnp.sum(k_refs[hh * n_fetch + t][0, 0] * qh, axis=1, keepdims=True)
                  for t in range(n_fetch)]
        m = s_self
        for s in scores:
            m = jnp.maximum(m, jnp.max(s, axis=0, keepdims=True))
        p_self = jnp.exp(s_self - m)
        l = p_self
        acc = p_self * vn_ref[0]
        for t, s in enumerate(scores):
            p = jnp.exp(s - m)
            l = l + jnp.sum(p, axis=0, keepdims=True)
            acc = acc + jnp.sum(p * v_refs[hh * n_fetch + t][0, 0], axis=0, keepdims=True)
        out = jnp.where(mine, acc / l, out)
    o_ref[0] = out.astype(BF16)


def _sample_attn(cache_k_layer, cache_v_layer, page_table, idx, q3, kn3, vn3, *, n_heads):
    n_pool, page_size, width = cache_k_layer.shape
    b = page_table.shape[0]
    ppb = MOBA_BLOCK // page_size
    n_fetch = MOBA_TOPK * ppb
    ck = cache_k_layer.reshape(1, n_pool, page_size, width)
    cv = cache_v_layer.reshape(1, n_pool, page_size, width)

    def page_spec(hh, r, t):
        def index(bi, hp, idx_s, pt):
            blk = idx_s[(bi * MOBA_TOPK + r) * n_heads + 2 * hp + hh]
            return (0, pt[bi, blk * ppb + t], 0, hp)
        return pl.BlockSpec((1, 1, page_size, LANES), index)

    fetch = [page_spec(hh, r, t) for hh in range(2) for r in range(MOBA_TOPK) for t in range(ppb)]
    vec = pl.BlockSpec((1, 1, LANES), lambda bi, hp, idx_s, pt: (bi, 0, hp))
    return pl.pallas_call(
        functools.partial(_sattn_kernel, n_fetch=n_fetch),
        grid_spec=pltpu.PrefetchScalarGridSpec(
            num_scalar_prefetch=2,
            grid=(b, n_heads // 2),
            in_specs=fetch + fetch + [vec, vec, vec],
            out_specs=vec,
        ),
        out_shape=jax.ShapeDtypeStruct((b, 1, width), BF16),
        compiler_params=_params(("parallel", "arbitrary")),
        name="sample_attn",
    )(idx.reshape(-1), page_table, *([ck] * len(fetch)), *([cv] * len(fetch)), q3, kn3, vn3)


def kernel(x_prompt, x_sample, cache_k, cache_v, page_table, state_pool, state_conv, norm_ffn1, ffn1_gate, ffn1_up, ffn1_down, norm_mix, w_in, w_pool, pool_scale, conv_w, w_o, norm_ffn2, ffn2_gate, ffn2_up, ffn2_down, norm_final):
    b, s, d = x_prompt.shape
    db, t_new, _ = x_sample.shape
    depth, n_pool, page_size, n_heads, head_dim = cache_k.shape
    assert head_dim == HEAD_DIM and t_new == 1 and n_heads % 2 == 0
    past_len = page_table.shape[1] * page_size
    assert past_len % MOBA_BLOCK == 0 and past_len + 1 >= max(POOL_WINDOWS)
    wa = n_heads * HEAD_DIM
    tm_prompt = 512 if (b * s) % 512 == 0 else MOBA_BLOCK

    cos_p, sin_p = _rope_tables(jnp.arange(s, dtype=jnp.int32))
    cos_s, sin_s = _rope_tables(past_len + jnp.arange(1, dtype=jnp.int32))
    row = lambda a: a.reshape(1, -1)
    bf = lambda a: a.astype(BF16)

    xp = x_prompt.reshape(b * s, d)
    xs = x_sample.reshape(db, d)
    mix_p = mix_s = None
    outs = {n: [] for n in ("kp", "vp", "ks", "vs", "pp", "ps", "cp", "cs")}
    for l in range(depth):
        n_groups = w_pool.shape[1]
        wpool_bd = bf(jax.scipy.linalg.block_diag(*[w_pool[l, g] for g in range(n_groups)]))
        f1 = (row(norm_ffn1[l]), bf(ffn1_gate[l]), bf(ffn1_up[l]), bf(ffn1_down[l]))
        f2 = (row(norm_ffn2[l]), bf(ffn2_gate[l]), bf(ffn2_up[l]), bf(ffn2_down[l]))
        mixer_w = (row(norm_mix[l]), bf(w_in[l]))
        mixer_tail = (wpool_bd, row(pool_scale[l]), conv_w[l])
        wo = bf(w_o[l])

        xp = _ffn(xp, *f1, tm=tm_prompt)
        kp, vp, q, kt, va, pc, kmean, ptail, ctail = _proj_prompt(
            xp.reshape(b, s, d), *mixer_w, cos_p, sin_p, *mixer_tail, n_heads=n_heads)
        a = _attn_prompt(q, kt, va, _gate_table(kmean, n_heads))
        last = l == depth - 1
        xp = _ffn(xp, *f2, mix=(a.reshape(b * s, wa), pc.reshape(b * s, -1), wo),
                  final=row(norm_final) if last else None, tm=tm_prompt)
        outs["kp"].append(kp.reshape(b, s, n_heads, HEAD_DIM))
        outs["vp"].append(vp.reshape(b, s, n_heads, HEAD_DIM))
        outs["pp"].append(ptail[:, POOL_CARRY - POOL_STATE:])
        outs["cp"].append(ctail[:, CONV_CARRY - (CONV_K - 1):])

        xs = _ffn(xs, *f1, tm=db)
        qs, ksn, vsn, pcs, us, zs = _proj_sample(
            xs, *mixer_w, cos_s, sin_s, *mixer_tail, state_pool[l], state_conv[l],
            n_heads=n_heads, pos=past_len)
        ck = cache_k[l].reshape(n_pool, page_size, wa)
        cv = cache_v[l].reshape(n_pool, page_size, wa)
        q3, kn3, vn3 = qs.reshape(db, 1, wa), ksn.reshape(db, 1, wa), vsn.reshape(db, 1, wa)
        idx = _sample_gates(ck, page_table, q3, n_heads=n_heads)
        a_s = _sample_attn(ck, cv, page_table, idx[:, :MOBA_TOPK, :n_heads], q3, kn3, vn3,
                           n_heads=n_heads)
        xs = _ffn(xs, *f2, mix=(a_s.reshape(db, wa), pcs, wo),
                  final=row(norm_final) if last else None, tm=db)
        outs["ks"].append(ksn.reshape(db, 1, n_heads, HEAD_DIM))
        outs["vs"].append(vsn.reshape(db, 1, n_heads, HEAD_DIM))
        outs["ps"].append(jnp.concatenate([state_pool[l][:, 1:], us[:, None]], axis=1))
        outs["cs"].append(jnp.concatenate([state_conv[l][:, 1:], zs[:, None]], axis=1))

    st = lambda n: jnp.stack(outs[n])
    return (xp.reshape(b, s, d), xs.reshape(db, 1, d), st("kp"), st("vp"), st("ks"), st("vs"),
            st("pp"), st("ps"), st("cp"), st("cs"))
```

```python
import functools

import jax
import jax.numpy as jnp
from jax import lax
from jax.experimental import pallas as pl
from jax.experimental.pallas import tpu as pltpu

F32 = jnp.float32
BF16 = jnp.bfloat16

HEAD_DIM = 64
LANES = 128
MOBA_BLOCK = 256
MOBA_TOPK = 3
POOL_WINDOWS = (2, 4, 8, 16)
POOL_STATE = max(POOL_WINDOWS) - 1
POOL_CARRY = 16
CONV_K = 3
CONV_CARRY = 8
ROPE_THETA = 10000.0
RMS_EPS = 1e-6
NEG = -1e30
VMEM_LIMIT = 56 * 1024 * 1024


def _dot(a, b):
    return jnp.dot(a, b, preferred_element_type=F32)


def _rms(x, g):
    return x * lax.rsqrt(jnp.mean(x * x, axis=-1, keepdims=True) + RMS_EPS) * g


def _resident(shape):
    return pl.BlockSpec(shape, lambda *_: (0,) * len(shape), pipeline_mode=pl.Buffered(1))


def _params(sem):
    return pltpu.CompilerParams(dimension_semantics=sem, vmem_limit_bytes=VMEM_LIMIT)


def _ffn_kernel(*refs, premix, final_norm, ff_chunks):
    it = iter(refs)
    x_ref = next(it)
    if premix:
        a_ref, pc_ref, wo_ref = next(it), next(it), next(it)
    n_ref, wg_ref, wu_ref, wd_ref = next(it), next(it), next(it), next(it)
    if final_norm:
        nf_ref = next(it)
    o_ref = next(it)

    x = x_ref[...]
    if premix:
        wa = a_ref.shape[-1]
        x = x + _dot(a_ref[...], wo_ref[:wa, :]) + _dot(pc_ref[...], wo_ref[wa:, :])
    h = _rms(x, n_ref[...]).astype(BF16)
    y = jnp.zeros_like(x)
    for lo, hi in ff_chunks:
        g = _dot(h, wg_ref[:, lo:hi])
        u = _dot(h, wu_ref[:, lo:hi])
        act = (g * jax.nn.sigmoid(g) * u).astype(BF16)
        y = y + _dot(act, wd_ref[lo:hi, :])
    x = x + 0.5 * y
    if final_norm:
        x = _rms(x, nf_ref[...])
    o_ref[...] = x


def _ffn(x, norm, wg, wu, wd, *, mix=None, final=None, tm):
    m, d = x.shape
    ff = wg.shape[1]
    half = (ff // 2) // LANES * LANES
    ff_chunks = ((0, half), (half, ff))
    row = lambda w: pl.BlockSpec((tm, w), lambda i: (i, 0))
    args, specs = [x], [row(d)]
    if mix is not None:
        a, pc, wo = mix
        args += [a, pc, wo]
        specs += [row(a.shape[1]), row(pc.shape[1]), _resident(wo.shape)]
    args += [norm, wg, wu, wd]
    specs += [_resident(norm.shape), _resident(wg.shape), _resident(wu.shape), _resident(wd.shape)]
    if final is not None:
        args.append(final)
        specs.append(_resident(final.shape))
    return pl.pallas_call(
        functools.partial(_ffn_kernel, premix=mix is not None, final_norm=final is not None,
                          ff_chunks=ff_chunks),
        grid=(m // tm,),
        in_specs=specs,
        out_specs=row(d),
        out_shape=jax.ShapeDtypeStruct((m, d), F32),
        compiler_params=_params(("parallel",)),
        name="ffn",
    )(*args)


def _rope_chunk(x, cos, sin_signed):
    lane = lax.broadcasted_iota(jnp.int32, x.shape, 1)
    first_half = (lane % HEAD_DIM) < (HEAD_DIM // 2)
    partner = jnp.where(first_half, pltpu.roll(x, LANES - HEAD_DIM // 2, 1),
                        pltpu.roll(x, HEAD_DIM // 2, 1))
    return x * cos + partner * sin_signed


def _pool_select(s2, s4, s8, s16, pos):
    lane = lax.broadcasted_iota(jnp.int32, s2.shape, 1)
    group = s2.shape[1] // len(POOL_WINDOWS)
    posf = (pos + 1).astype(F32)
    mean = lambda s, w: s / jnp.minimum(posf, float(w))
    return jnp.where(lane < group, mean(s2, 2),
                     jnp.where(lane < 2 * group, mean(s4, 4),
                               jnp.where(lane < 3 * group, mean(s8, 8), mean(s16, 16))))


def _rope_tables(pos):
    half = HEAD_DIM // 2
    inv = jnp.power(ROPE_THETA, -jnp.arange(half, dtype=F32) / half)
    ang = pos.astype(F32)[:, None] * inv[None, :]
    cos, sin = jnp.cos(ang), jnp.sin(ang)
    reps = LANES // HEAD_DIM
    return (jnp.tile(jnp.concatenate([cos, cos], axis=1), (1, reps)),
            jnp.tile(jnp.concatenate([-sin, sin], axis=1), (1, reps)))


def _proj_kernel(x_ref, nm_ref, win_ref, cos_ref, sin_ref, wpool_ref, pscale_ref, convw_ref,
                 k_ref, v_ref, q_ref, kt_ref, va_ref, pc_ref, km_ref, ptail_ref, ctail_ref,
                 prevu_ref, prevz_ref, *, wa, wb, wc):
    i = pl.program_id(1)
    tm = x_ref.shape[1]
    n_chunks = wa // LANES

    @pl.when(i == 0)
    def _():
        prevu_ref[...] = jnp.zeros_like(prevu_ref)
        prevz_ref[...] = jnp.zeros_like(prevz_ref)

    h = _rms(x_ref[0], nm_ref[...]).astype(BF16)
    cos, sin = cos_ref[...], sin_ref[...]
    lane = lax.broadcasted_iota(jnp.int32, (tm, LANES), 1)
    low = lane < HEAD_DIM
    ones_col = (lane == HEAD_DIM).astype(F32)
    block_onehot_t = (lax.broadcasted_iota(jnp.int32, (HEAD_DIM, tm), 0) == i).astype(BF16)
    scale = HEAD_DIM ** -0.5

    for c in range(n_chunks):
        sl = slice(c * LANES, (c + 1) * LANES)
        q = _rope_chunk(_dot(h, win_ref[:, sl]), cos, sin) * scale
        k = _rope_chunk(_dot(h, win_ref[:, wa + c * LANES: wa + (c + 1) * LANES]), cos, sin)
        v = _dot(h, win_ref[:, 2 * wa + c * LANES: 2 * wa + (c + 1) * LANES])
        km_ref[0, 0, :, sl] = jnp.mean(k, axis=0, keepdims=True)
        k_t, v_t = k.T, v.T
        for hh in range(2):
            head = 2 * c + hh
            rows = slice(hh * HEAD_DIM, (hh + 1) * HEAD_DIM)
            k_ref[0, head] = k_t[rows]
            v_ref[0, head] = v_t[rows]
            kt_ref[0, head, 0] = jnp.concatenate([k_t[rows].astype(BF16), block_onehot_t], axis=0)
            qh, vh = (q, v) if hh == 0 else (pltpu.roll(q, HEAD_DIM, 1), pltpu.roll(v, HEAD_DIM, 1))
            q_ref[0, head] = jnp.where(low, qh, 0.0).astype(BF16)
            va_ref[0, head, 0] = jnp.where(low, vh, ones_col).astype(BF16)

    o = 3 * wa
    u = _dot(h, win_ref[:, o:o + wb])
    hc = _dot(h, win_ref[:, o + wb:o + wb + wc])
    bg = _dot(h, win_ref[:, o + wb + wc:o + wb + 2 * wc])
    cg = _dot(h, win_ref[:, o + wb + 2 * wc:o + wb + 3 * wc])

    pos = i * tm + lax.broadcasted_iota(jnp.int32, (tm, 1), 0)
    e = jnp.concatenate([prevu_ref[...], u], axis=0)
    s2 = e + pltpu.roll(e, 1, 0)
    s4 = s2 + pltpu.roll(s2, 2, 0)
    s8 = s4 + pltpu.roll(s4, 4, 0)
    s16 = s8 + pltpu.roll(s8, 8, 0)
    t = slice(POOL_CARRY, POOL_CARRY + tm)
    d = _pool_select(s2[t], s4[t], s8[t], s16[t], pos) - u
    p = _dot(d.astype(BF16), wpool_ref[...]) * pscale_ref[...]
    prevu_ref[...] = u[tm - POOL_CARRY:]
    ptail_ref[0] = u[tm - POOL_CARRY:]

    z = cg * hc
    ze = jnp.concatenate([prevz_ref[...], z], axis=0)
    tz = slice(CONV_CARRY, CONV_CARRY + tm)
    conv = (z * convw_ref[2:3, :] + pltpu.roll(ze, 1, 0)[tz] * convw_ref[1:2, :]
            + pltpu.roll(ze, 2, 0)[tz] * convw_ref[0:1, :])
    prevz_ref[...] = z[tm - CONV_CARRY:]
    ctail_ref[0] = z[tm - CONV_CARRY:]
    pc_ref[0, :, :wb] = p.astype(BF16)
    pc_ref[0, :, wb:] = (bg * conv).astype(BF16)


def _proj_prompt(x, nm, w_in, cos, sin, wpool_bd, pscale, conv_w, *, n_heads):
    b, s, d = x.shape
    tm = MOBA_BLOCK
    nb = s // tm
    wa = n_heads * HEAD_DIM
    wb = wpool_bd.shape[0]
    wc = conv_w.shape[1]
    assert s % tm == 0 and nb <= LANES - HEAD_DIM and wa % LANES == 0
    assert w_in.shape[1] == 3 * wa + wb + 3 * wc
    seq = lambda w: pl.BlockSpec((1, tm, w), lambda bi, i: (bi, i, 0))
    tab = pl.BlockSpec((tm, LANES), lambda bi, i: (i, 0))
    per_b = lambda r, w: pl.BlockSpec((1, r, w), lambda bi, i: (bi, 0, 0))
    head_t = pl.BlockSpec((1, n_heads, HEAD_DIM, tm), lambda bi, i: (bi, 0, 0, i))
    out_shape = (
        jax.ShapeDtypeStruct((b, n_heads, HEAD_DIM, s), F32),
        jax.ShapeDtypeStruct((b, n_heads, HEAD_DIM, s), F32),
        jax.ShapeDtypeStruct((b, n_heads, s, LANES), BF16),
        jax.ShapeDtypeStruct((b, n_heads, nb, LANES, tm), BF16),
        jax.ShapeDtypeStruct((b, n_heads, nb, tm, LANES), BF16),
        jax.ShapeDtypeStruct((b, s, wb + wc), BF16),
        jax.ShapeDtypeStruct((b, nb, 1, wa), F32),
        jax.ShapeDtypeStruct((b, POOL_CARRY, wb), F32),
        jax.ShapeDtypeStruct((b, CONV_CARRY, wc), F32),
    )
    out_specs = (
        head_t, head_t,
        pl.BlockSpec((1, n_heads, tm, LANES), lambda bi, i: (bi, 0, i, 0)),
        pl.BlockSpec((1, n_heads, 1, LANES, tm), lambda bi, i: (bi, 0, i, 0, 0)),
        pl.BlockSpec((1, n_heads, 1, tm, LANES), lambda bi, i: (bi, 0, i, 0, 0)),
        seq(wb + wc), pl.BlockSpec((1, 1, 1, wa), lambda bi, i: (bi, i, 0, 0)),
        per_b(POOL_CARRY, wb), per_b(CONV_CARRY, wc),
    )
    return pl.pallas_call(
        functools.partial(_proj_kernel, wa=wa, wb=wb, wc=wc),
        grid=(b, nb),
        in_specs=[seq(d), _resident(nm.shape), _resident(w_in.shape), tab, tab,
                  _resident(wpool_bd.shape), _resident(pscale.shape), _resident(conv_w.shape)],
        out_specs=out_specs,
        out_shape=out_shape,
        scratch_shapes=[pltpu.VMEM((POOL_CARRY, wb), F32), pltpu.VMEM((CONV_CARRY, wc), F32)],
        compiler_params=_params(("parallel", "arbitrary")),
        name="proj_prompt",
    )(x, nm, w_in, cos, sin, wpool_bd, pscale, conv_w)


def _top_blocks(gate, n_valid):
    lane = lax.broadcasted_iota(jnp.int32, gate.shape, 1)
    lane_f = lane.astype(F32)
    valid = (lane >= HEAD_DIM) & (lane - HEAD_DIM < n_valid)
    g = jnp.where(valid, gate, -jnp.inf)
    picked = jnp.zeros(gate.shape, jnp.bool_)
    for _ in range(MOBA_TOPK):
        m = jnp.max(g, axis=1, keepdims=True)
        first = jnp.min(jnp.where(g == m, lane_f, float(2 * LANES)), axis=1, keepdims=True)
        hit = (lane_f == first) & (m > -jnp.inf)
        picked = picked | hit
        g = jnp.where(hit, -jnp.inf, g)
    return picked


def _attn_kernel(q_ref, kt_ref, va_ref, kmt_ref, o_ref):
    i = pl.program_id(2)
    tq = q_ref.shape[2]
    lane = lax.broadcasted_iota(jnp.int32, (tq, LANES), 1)
    row = lax.broadcasted_iota(jnp.int32, (tq, MOBA_BLOCK), 0)
    col = lax.broadcasted_iota(jnp.int32, (tq, MOBA_BLOCK), 1)
    q_sel, state = [], []
    for hh in range(2):
        q = q_ref[0, hh]
        km = kmt_ref[0, hh]
        km_hi = km.astype(BF16)
        km_lo = (km - km_hi.astype(F32)).astype(BF16)
        picked = _top_blocks(_dot(q, km_hi) + _dot(q, km_lo), i)
        q_sel.append(jnp.where((lane >= HEAD_DIM) & jnp.logical_not(picked), NEG,
                               q.astype(F32)).astype(BF16))
        s = jnp.where(col <= row, _dot(q, kt_ref[0, hh, i]), NEG)
        m = jnp.max(s, axis=1, keepdims=True)
        state += [m, _dot(jnp.exp(s - m).astype(BF16), va_ref[0, hh, i])]

    last_pair = kt_ref.shape[2] // 2 - 1

    def scores(jj, hh):
        return [_dot(q_sel[hh], kt_ref[0, hh, 2 * jj]), _dot(q_sel[hh], kt_ref[0, hh, 2 * jj + 1])]

    def body(jj, carry):
        new = []
        nxt = jnp.minimum(jj + 1, last_pair)
        for hh in range(2):
            m, acc, s0, s1 = carry[4 * hh:4 * hh + 4]
            s_next = scores(nxt, hh)
            m_new = jnp.maximum(m, jnp.max(jnp.maximum(s0, s1), axis=1, keepdims=True))
            p0 = jnp.exp(s0 - m_new).astype(BF16)
            p1 = jnp.exp(s1 - m_new).astype(BF16)
            acc = (acc * jnp.exp(m - m_new) + _dot(p0, va_ref[0, hh, 2 * jj])
                   + _dot(p1, va_ref[0, hh, 2 * jj + 1]))
            new += [m_new, acc] + s_next
        return tuple(new)

    init = tuple(state[0:2] + scores(0, 0) + state[2:4] + scores(0, 1))
    state = lax.fori_loop(0, (i + 1) // 2, body, init)
    outs = [state[4 * hh + 1] / state[4 * hh + 1][:, HEAD_DIM:HEAD_DIM + 1] for hh in range(2)]
    o_ref[0] = jnp.where(lane < HEAD_DIM, outs[0], pltpu.roll(outs[1], HEAD_DIM, 1)).astype(BF16)


def _attn_prompt(q, kt, va, kmt):
    b, n_heads, s, _ = q.shape
    nb = kt.shape[2]
    tq = MOBA_BLOCK
    assert nb % 2 == 0
    return pl.pallas_call(
        _attn_kernel,
        grid=(b, n_heads // 2, nb),
        in_specs=[
            pl.BlockSpec((1, 2, tq, LANES), lambda bi, hp, i: (bi, hp, i, 0)),
            pl.BlockSpec((1, 2, nb, LANES, tq), lambda bi, hp, i: (bi, hp, 0, 0, 0)),
            pl.BlockSpec((1, 2, nb, tq, LANES), lambda bi, hp, i: (bi, hp, 0, 0, 0)),
            pl.BlockSpec((1, 2, LANES, LANES), lambda bi, hp, i: (bi, hp, 0, 0)),
        ],
        out_specs=pl.BlockSpec((1, tq, LANES), lambda bi, hp, i: (bi, i, hp)),
        out_shape=jax.ShapeDtypeStruct((b, s, n_heads * HEAD_DIM), BF16),
        compiler_params=_params(("parallel", "parallel", "arbitrary")),
        name="attn_prompt",
    )(q, kt, va, kmt)


def _gate_table(kmean, n_heads):
    b, nb = kmean.shape[:2]
    t = kmean.reshape(b, nb, n_heads, HEAD_DIM).transpose(0, 2, 3, 1)
    return jnp.pad(t, ((0, 0), (0, 0), (0, LANES - HEAD_DIM), (HEAD_DIM, LANES - HEAD_DIM - nb)))


def _sproj_kernel(x_ref, nm_ref, win_ref, cos_ref, sin_ref, wpool_ref, pscale_ref, convw_ref,
                  spool_ref, sconv_ref, q_ref, k_ref, v_ref, pc_ref, u_ref, z_ref,
                  *, wa, wb, wc, pos):
    h = _rms(x_ref[...], nm_ref[...]).astype(BF16)
    cos, sin = cos_ref[...], sin_ref[...]
    scale = HEAD_DIM ** -0.5
    for c in range(wa // LANES):
        sl = slice(c * LANES, (c + 1) * LANES)
        q_ref[:, sl] = _rope_chunk(_dot(h, win_ref[:, sl]), cos, sin) * scale
        k_ref[:, sl] = _rope_chunk(_dot(h, win_ref[:, wa + c * LANES: wa + (c + 1) * LANES]), cos, sin)
        v_ref[:, sl] = _dot(h, win_ref[:, 2 * wa + c * LANES: 2 * wa + (c + 1) * LANES])
    o = 3 * wa
    u = _dot(h, win_ref[:, o:o + wb])
    hc = _dot(h, win_ref[:, o + wb:o + wb + wc])
    bg = _dot(h, win_ref[:, o + wb + wc:o + wb + 2 * wc])
    cg = _dot(h, win_ref[:, o + wb + 2 * wc:o + wb + 3 * wc])

    sums, run = {}, u
    for r in range(1, max(POOL_WINDOWS)):
        run = run + spool_ref[:, POOL_STATE - r, :]
        if r + 1 in POOL_WINDOWS:
            sums[r + 1] = run
    posv = jnp.full((u.shape[0], 1), pos, jnp.int32)
    d = _pool_select(sums[2], sums[4], sums[8], sums[16], posv) - u
    p = _dot(d.astype(BF16), wpool_ref[...]) * pscale_ref[...]

    z = cg * hc
    conv = (z * convw_ref[2:3, :] + sconv_ref[:, 1, :] * convw_ref[1:2, :]
            + sconv_ref[:, 0, :] * convw_ref[0:1, :])
    u_ref[...] = u
    z_ref[...] = z
    pc_ref[:, :wb] = p.astype(BF16)
    pc_ref[:, wb:] = (bg * conv).astype(BF16)


def _proj_sample(x, nm, w_in, cos, sin, wpool_bd, pscale, conv_w, spool, sconv, *, n_heads, pos):
    m, _ = x.shape
    wa = n_heads * HEAD_DIM
    wb = wpool_bd.shape[0]
    wc = conv_w.shape[1]
    assert spool.shape[1] == POOL_STATE and sconv.shape[1] == CONV_K - 1
    f = lambda w, dt: jax.ShapeDtypeStruct((m, w), dt)
    return pl.pallas_call(
        functools.partial(_sproj_kernel, wa=wa, wb=wb, wc=wc, pos=pos),
        out_shape=(f(wa, F32), f(wa, F32), f(wa, F32), f(wb + wc, BF16), f(wb, F32), f(wc, F32)),
        compiler_params=pltpu.CompilerParams(vmem_limit_bytes=VMEM_LIMIT),
        name="proj_sample",
    )(x, nm, w_in, cos, sin, wpool_bd, pscale, conv_w, spool, sconv)


PAGES_PER_STEP = 16


def _sgate_kernel(pt_ref, *refs, pages_per_block, n_blocks):
    page_refs = refs[:PAGES_PER_STEP]
    q_ref, idx_ref, gate_ref = refs[PAGES_PER_STEP:]
    g = pl.program_id(1)
    blocks_per_step = PAGES_PER_STEP // pages_per_block
    n_heads = q_ref.shape[1]
    lane = lax.broadcasted_iota(jnp.int32, (n_heads, LANES), 1)

    @pl.when(g == 0)
    def _():
        gate_ref[...] = jnp.zeros_like(gate_ref)

    q = jnp.broadcast_to(q_ref[0], page_refs[0].shape[2:])
    gate = gate_ref[...]
    for r in range(blocks_per_step):
        tot = page_refs[r * pages_per_block][0, 0]
        for t in range(1, pages_per_block):
            tot = tot + page_refs[r * pages_per_block + t][0, 0]
        per_token = jnp.sum(tot * q, axis=1)
        val = jnp.sum(per_token, axis=1, keepdims=True) / float(MOBA_BLOCK)
        gate = jnp.where(lane == g * blocks_per_step + r, val, gate)
    gate_ref[...] = gate

    @pl.when(g == pl.num_programs(1) - 1)
    def _():
        lane_f = lane.astype(F32)
        gt = jnp.where(lane < n_blocks, gate, -jnp.inf)
        picks = jnp.zeros((n_heads, LANES), F32)
        for r in range(MOBA_TOPK):
            mx = jnp.max(gt, axis=1, keepdims=True)
            first = jnp.min(jnp.where(gt == mx, lane_f, float(LANES)), axis=1, keepdims=True)
            gt = jnp.where(lane_f == first, -jnp.inf, gt)
            picks = jnp.where(lane == r, first, picks)
        idx_ref[0] = picks.astype(jnp.int32)


def _sample_gates(cache_kt, page_table, q4, *, layer):
    _, n_pool, n_heads, _, page_size = cache_kt.shape
    b, n_pages = page_table.shape
    ppb = MOBA_BLOCK // page_size
    nbp = n_pages // ppb
    assert MOBA_BLOCK % page_size == 0 and PAGES_PER_STEP % ppb == 0
    assert n_pages % PAGES_PER_STEP == 0 and MOBA_TOPK <= nbp <= LANES
    page_spec = lambda r: pl.BlockSpec(
        (1, 1, n_heads, HEAD_DIM, page_size),
        lambda bi, g, pt: (layer, pt[bi, g * PAGES_PER_STEP + r], 0, 0, 0))
    return pl.pallas_call(
        functools.partial(_sgate_kernel, pages_per_block=ppb, n_blocks=nbp),
        grid_spec=pltpu.PrefetchScalarGridSpec(
            num_scalar_prefetch=1,
            grid=(b, n_pages // PAGES_PER_STEP),
            in_specs=[page_spec(r) for r in range(PAGES_PER_STEP)]
            + [pl.BlockSpec((1, n_heads, HEAD_DIM, 1), lambda bi, g, pt: (bi, 0, 0, 0))],
            out_specs=pl.BlockSpec((1, n_heads, LANES), lambda bi, g, pt: (bi, 0, 0)),
            scratch_shapes=[pltpu.VMEM((n_heads, LANES), F32)],
        ),
        out_shape=jax.ShapeDtypeStruct((b, n_heads, LANES), jnp.int32),
        compiler_params=_params(("parallel", "arbitrary")),
        name="sample_gates",
    )(page_table, *([cache_kt] * PAGES_PER_STEP), q4)


def _sattn_kernel(idx_ref, pt_ref, *refs, n_fetch):
    k_refs = refs[:2 * n_fetch]
    v_refs = refs[2 * n_fetch:4 * n_fetch]
    q_ref, kn_ref, vn_ref, o_ref = refs[4 * n_fetch:]
    for hh in range(2):
        q = q_ref[0, hh]
        s_self = jnp.sum(q * kn_ref[0, hh], axis=0, keepdims=True)
        scores = [jnp.sum(k_refs[hh * n_fetch + t][0, 0, 0] * q, axis=0, keepdims=True)
                  for t in range(n_fetch)]
        m = s_self
        for s in scores:
            m = jnp.maximum(m, jnp.max(s, axis=1, keepdims=True))
        p_self = jnp.exp(s_self - m)
        l = p_self
        acc = p_self * vn_ref[0, hh]
        for t, s in enumerate(scores):
            p = jnp.exp(s - m)
            l = l + jnp.sum(p, axis=1, keepdims=True)
            acc = acc + jnp.sum(v_refs[hh * n_fetch + t][0, 0, 0] * p, axis=1, keepdims=True)
        o_ref[0, hh] = acc / l


def _sample_attn(cache_kt, cache_vt, page_table, idx, q4, kn4, vn4, *, layer):
    _, n_pool, n_heads, _, page_size = cache_kt.shape
    b = page_table.shape[0]
    ppb = MOBA_BLOCK // page_size
    n_fetch = MOBA_TOPK * ppb

    def page_spec(hh, r, t):
        def index(bi, hp, idx_s, pt):
            head = 2 * hp + hh
            blk = idx_s[(bi * n_heads + head) * MOBA_TOPK + r]
            return (layer, pt[bi, blk * ppb + t], head, 0, 0)
        return pl.BlockSpec((1, 1, 1, HEAD_DIM, page_size), index)

    fetch = [page_spec(hh, r, t) for hh in range(2) for r in range(MOBA_TOPK) for t in range(ppb)]
    vec = pl.BlockSpec((1, 2, HEAD_DIM, 1), lambda bi, hp, idx_s, pt: (bi, hp, 0, 0))
    return pl.pallas_call(
        functools.partial(_sattn_kernel, n_fetch=n_fetch),
        grid_spec=pltpu.PrefetchScalarGridSpec(
            num_scalar_prefetch=2,
            grid=(b, n_heads // 2),
            in_specs=fetch + fetch + [vec, vec, vec],
            out_specs=vec,
        ),
        out_shape=jax.ShapeDtypeStruct((b, n_heads, HEAD_DIM, 1), F32),
        compiler_params=_params(("parallel", "arbitrary")),
        name="sample_attn",
    )(idx.reshape(-1), page_table, *([cache_kt] * len(fetch)), *([cache_vt] * len(fetch)),
      q4, kn4, vn4)


def kernel(x_prompt, x_sample, cache_k, cache_v, page_table, state_pool, state_conv, norm_ffn1, ffn1_gate, ffn1_up, ffn1_down, norm_mix, w_in, w_pool, pool_scale, conv_w, w_o, norm_ffn2, ffn2_gate, ffn2_up, ffn2_down, norm_final):
    b, s, d = x_prompt.shape
    db, t_new, _ = x_sample.shape
    depth, n_pool, page_size, n_heads, head_dim = cache_k.shape
    assert head_dim == HEAD_DIM and t_new == 1 and n_heads % 2 == 0
    past_len = page_table.shape[1] * page_size
    assert past_len % MOBA_BLOCK == 0 and past_len + 1 >= max(POOL_WINDOWS)
    wa = n_heads * HEAD_DIM
    tm_prompt = 512 if (b * s) % 512 == 0 else MOBA_BLOCK

    cos_p, sin_p = _rope_tables(jnp.arange(s, dtype=jnp.int32))
    cos_s, sin_s = _rope_tables(past_len + jnp.arange(1, dtype=jnp.int32))
    row = lambda a: a.reshape(1, -1)
    bf = lambda a: a.astype(BF16)

    cache_kt = cache_k.transpose(0, 1, 3, 4, 2)
    cache_vt = cache_v.transpose(0, 1, 3, 4, 2)

    xp = x_prompt.reshape(b * s, d)
    xs = x_sample.reshape(db, d)
    outs = {n: [] for n in ("kp", "vp", "ks", "vs", "pp", "ps", "cp", "cs")}
    for l in range(depth):
        n_groups = w_pool.shape[1]
        wpool_bd = bf(jax.scipy.linalg.block_diag(*[w_pool[l, g] for g in range(n_groups)]))
        f1 = (row(norm_ffn1[l]), bf(ffn1_gate[l]), bf(ffn1_up[l]), bf(ffn1_down[l]))
        f2 = (row(norm_ffn2[l]), bf(ffn2_gate[l]), bf(ffn2_up[l]), bf(ffn2_down[l]))
        mixer_w = (row(norm_mix[l]), bf(w_in[l]))
        mixer_tail = (wpool_bd, row(pool_scale[l]), conv_w[l])
        wo = bf(w_o[l])

        xp = _ffn(xp, *f1, tm=tm_prompt)
        kp, vp, q, kt, va, pc, kmean, ptail, ctail = _proj_prompt(
            xp.reshape(b, s, d), *mixer_w, cos_p, sin_p, *mixer_tail, n_heads=n_heads)
        a = _attn_prompt(q, kt, va, _gate_table(kmean, n_heads))
        last = l == depth - 1
        xp = _ffn(xp, *f2, mix=(a.reshape(b * s, wa), pc.reshape(b * s, -1), wo),
                  final=row(norm_final) if last else None, tm=tm_prompt)
        outs["kp"].append(kp.transpose(0, 3, 1, 2))
        outs["vp"].append(vp.transpose(0, 3, 1, 2))
        outs["pp"].append(ptail[:, POOL_CARRY - POOL_STATE:])
        outs["cp"].append(ctail[:, CONV_CARRY - (CONV_K - 1):])

        xs = _ffn(xs, *f1, tm=db)
        qs, ksn, vsn, pcs, us, zs = _proj_sample(
            xs, *mixer_w, cos_s, sin_s, *mixer_tail, state_pool[l], state_conv[l],
            n_heads=n_heads, pos=past_len)
        col = lambda a: a.reshape(db, n_heads, HEAD_DIM, 1)
        idx = _sample_gates(cache_kt, page_table, col(qs), layer=l)
        a_s = _sample_attn(cache_kt, cache_vt, page_table, idx[:, :, :MOBA_TOPK],
                           col(qs), col(ksn), col(vsn), layer=l)
        xs = _ffn(xs, *f2, mix=(bf(a_s.reshape(db, wa)), pcs, wo),
                  final=row(norm_final) if last else None, tm=db)
        outs["ks"].append(ksn.reshape(db, 1, n_heads, HEAD_DIM))
        outs["vs"].append(vsn.reshape(db, 1, n_heads, HEAD_DIM))
        outs["ps"].append(jnp.concatenate([state_pool[l][:, 1:], us[:, None]], axis=1))
        outs["cs"].append(jnp.concatenate([state_conv[l][:, 1:], zs[:, None]], axis=1))

    st = lambda n: jnp.stack(outs[n])
    return (xp.reshape(b, s, d), xs.reshape(db, 1, d), st("kp"), st("vp"), st("ks"), st("vs"),
            st("pp"), st("ps"), st("cp"), st("cs"))
```

```python
import functools

import jax
import jax.numpy as jnp
from jax import lax
from jax.experimental import pallas as pl
from jax.experimental.pallas import tpu as pltpu

F32 = jnp.float32
BF16 = jnp.bfloat16

HEAD_DIM = 64
LANES = 128
MXU_DIM = 256
MOBA_BLOCK = 256
MOBA_TOPK = 3
POOL_WINDOWS = (2, 4, 8, 16)
POOL_STATE = max(POOL_WINDOWS) - 1
POOL_CARRY = 16
CONV_K = 3
CONV_CARRY = 8
ROPE_THETA = 10000.0
RMS_EPS = 1e-6
NEG = -1e30
VMEM_LIMIT = 56 * 1024 * 1024


def _dot(a, b):
    return jnp.dot(a, b, preferred_element_type=F32)


def _rms(x, g):
    return x * lax.rsqrt(jnp.mean(x * x, axis=-1, keepdims=True) + RMS_EPS) * g


def _resident(shape):
    return pl.BlockSpec(shape, lambda *_: (0,) * len(shape), pipeline_mode=pl.Buffered(1))


def _params(sem):
    return pltpu.CompilerParams(dimension_semantics=sem, vmem_limit_bytes=VMEM_LIMIT)


def _ffn_kernel(*refs, premix, final_norm, ff_chunks):
    it = iter(refs)
    x_ref = next(it)
    if premix:
        a_ref, pc_ref, wo_ref = next(it), next(it), next(it)
    n_ref, wg_ref, wu_ref, wd_ref = next(it), next(it), next(it), next(it)
    if final_norm:
        nf_ref = next(it)
    o_ref = next(it)

    x = x_ref[...]
    if premix:
        wa = a_ref.shape[-1]
        x = x + _dot(a_ref[...], wo_ref[:wa, :]) + _dot(pc_ref[...], wo_ref[wa:, :])
    h = _rms(x, n_ref[...]).astype(BF16)
    y = jnp.zeros_like(x)
    for lo, hi in ff_chunks:
        g = _dot(h, wg_ref[:, lo:hi])
        u = _dot(h, wu_ref[:, lo:hi])
        act = (g * jax.nn.sigmoid(g) * u).astype(BF16)
        y = y + _dot(act, wd_ref[lo:hi, :])
    x = x + 0.5 * y
    if final_norm:
        x = _rms(x, nf_ref[...])
    o_ref[...] = x


def _ffn(x, norm, wg, wu, wd, *, mix=None, final=None, tm):
    m, d = x.shape
    ff = wg.shape[1]
    half = -(-(ff // 2) // MXU_DIM) * MXU_DIM
    ff_chunks = ((0, half), (half, ff))
    row = lambda w: pl.BlockSpec((tm, w), lambda i: (i, 0))
    args, specs = [x], [row(d)]
    if mix is not None:
        a, pc, wo = mix
        args += [a, pc, wo]
        specs += [row(a.shape[1]), row(pc.shape[1]), _resident(wo.shape)]
    args += [norm, wg, wu, wd]
    specs += [_resident(norm.shape), _resident(wg.shape), _resident(wu.shape), _resident(wd.shape)]
    if final is not None:
        args.append(final)
        specs.append(_resident(final.shape))
    return pl.pallas_call(
        functools.partial(_ffn_kernel, premix=mix is not None, final_norm=final is not None,
                          ff_chunks=ff_chunks),
        grid=(m // tm,),
        in_specs=specs,
        out_specs=row(d),
        out_shape=jax.ShapeDtypeStruct((m, d), F32),
        compiler_params=_params(("parallel",)),
        name="ffn",
    )(*args)


def _rope_chunk(x, cos, sin_signed):
    lane = lax.broadcasted_iota(jnp.int32, x.shape, 1)
    first_half = (lane % HEAD_DIM) < (HEAD_DIM // 2)
    partner = jnp.where(first_half, pltpu.roll(x, LANES - HEAD_DIM // 2, 1),
                        pltpu.roll(x, HEAD_DIM // 2, 1))
    return x * cos + partner * sin_signed


def _pool_select(s2, s4, s8, s16, pos):
    lane = lax.broadcasted_iota(jnp.int32, s2.shape, 1)
    group = s2.shape[1] // len(POOL_WINDOWS)
    posf = (pos + 1).astype(F32)
    mean = lambda s, w: s / jnp.minimum(posf, float(w))
    return jnp.where(lane < group, mean(s2, 2),
                     jnp.where(lane < 2 * group, mean(s4, 4),
                               jnp.where(lane < 3 * group, mean(s8, 8), mean(s16, 16))))


def _rope_tables(pos):
    half = HEAD_DIM // 2
    inv = jnp.power(ROPE_THETA, -jnp.arange(half, dtype=F32) / half)
    ang = pos.astype(F32)[:, None] * inv[None, :]
    cos, sin = jnp.cos(ang), jnp.sin(ang)
    reps = LANES // HEAD_DIM
    return (jnp.tile(jnp.concatenate([cos, cos], axis=1), (1, reps)),
            jnp.tile(jnp.concatenate([-sin, sin], axis=1), (1, reps)))


def _proj_kernel(x_ref, nm_ref, win_ref, cos_ref, sin_ref, wpool_ref, pscale_ref, convw_ref,
                 k_ref, v_ref, q_ref, kt_ref, va_ref, pc_ref, km_ref, ptail_ref, ctail_ref,
                 prevu_ref, prevz_ref, *, wa, wb, wc):
    i = pl.program_id(1)
    tm = x_ref.shape[1]
    n_chunks = wa // LANES

    @pl.when(i == 0)
    def _():
        prevu_ref[...] = jnp.zeros_like(prevu_ref)
        prevz_ref[...] = jnp.zeros_like(prevz_ref)

    h = _rms(x_ref[0], nm_ref[...]).astype(BF16)
    cos, sin = cos_ref[...], sin_ref[...]
    lane = lax.broadcasted_iota(jnp.int32, (tm, LANES), 1)
    low = lane < HEAD_DIM
    ones_col = (lane == HEAD_DIM).astype(F32)
    block_onehot_t = (lax.broadcasted_iota(jnp.int32, (HEAD_DIM, tm), 0) == i).astype(BF16)
    scale = HEAD_DIM ** -0.5

    for c in range(n_chunks):
        sl = slice(c * LANES, (c + 1) * LANES)
        q = _rope_chunk(_dot(h, win_ref[:, sl]), cos, sin) * scale
        k = _rope_chunk(_dot(h, win_ref[:, wa + c * LANES: wa + (c + 1) * LANES]), cos, sin)
        v = _dot(h, win_ref[:, 2 * wa + c * LANES: 2 * wa + (c + 1) * LANES])
        km_ref[0, 0, :, sl] = jnp.mean(k, axis=0, keepdims=True)
        k_t, v_t = k.T, v.T
        for hh in range(2):
            head = 2 * c + hh
            rows = slice(hh * HEAD_DIM, (hh + 1) * HEAD_DIM)
            k_ref[0, head] = k_t[rows]
            v_ref[0, head] = v_t[rows]
            kt_ref[0, head, 0] = jnp.concatenate([k_t[rows].astype(BF16), block_onehot_t], axis=0)
            qh, vh = (q, v) if hh == 0 else (pltpu.roll(q, HEAD_DIM, 1), pltpu.roll(v, HEAD_DIM, 1))
            q_ref[0, head] = jnp.where(low, qh, 0.0).astype(BF16)
            va_ref[0, head, 0] = jnp.where(low, vh, ones_col).astype(BF16)

    o = 3 * wa
    u = _dot(h, win_ref[:, o:o + wb])
    hc = _dot(h, win_ref[:, o + wb:o + wb + wc])
    bg = _dot(h, win_ref[:, o + wb + wc:o + wb + 2 * wc])
    cg = _dot(h, win_ref[:, o + wb + 2 * wc:o + wb + 3 * wc])

    pos = i * tm + lax.broadcasted_iota(jnp.int32, (tm, 1), 0)
    e = jnp.concatenate([prevu_ref[...], u], axis=0)
    s2 = e + pltpu.roll(e, 1, 0)
    s4 = s2 + pltpu.roll(s2, 2, 0)
    s8 = s4 + pltpu.roll(s4, 4, 0)
    s16 = s8 + pltpu.roll(s8, 8, 0)
    t = slice(POOL_CARRY, POOL_CARRY + tm)
    d = _pool_select(s2[t], s4[t], s8[t], s16[t], pos) - u
    p = _dot(d.astype(BF16), wpool_ref[...]) * pscale_ref[...]
    prevu_ref[...] = u[tm - POOL_CARRY:]
    ptail_ref[0] = u[tm - POOL_CARRY:]

    z = cg * hc
    ze = jnp.concatenate([prevz_ref[...], z], axis=0)
    tz = slice(CONV_CARRY, CONV_CARRY + tm)
    conv = (z * convw_ref[2:3, :] + pltpu.roll(ze, 1, 0)[tz] * convw_ref[1:2, :]
            + pltpu.roll(ze, 2, 0)[tz] * convw_ref[0:1, :])
    prevz_ref[...] = z[tm - CONV_CARRY:]
    ctail_ref[0] = z[tm - CONV_CARRY:]
    pc_ref[0, :, :wb] = p.astype(BF16)
    pc_ref[0, :, wb:] = (bg * conv).astype(BF16)


def _proj_prompt(x, nm, w_in, cos, sin, wpool_bd, pscale, conv_w, *, n_heads):
    b, s, d = x.shape
    tm = MOBA_BLOCK
    nb = s // tm
    wa = n_heads * HEAD_DIM
    wb = wpool_bd.shape[0]
    wc = conv_w.shape[1]
    assert s % tm == 0 and nb <= LANES - HEAD_DIM and wa % LANES == 0
    assert w_in.shape[1] == 3 * wa + wb + 3 * wc
    seq = lambda w: pl.BlockSpec((1, tm, w), lambda bi, i: (bi, i, 0))
    tab = pl.BlockSpec((tm, LANES), lambda bi, i: (i, 0))
    per_b = lambda r, w: pl.BlockSpec((1, r, w), lambda bi, i: (bi, 0, 0))
    head_t = pl.BlockSpec((1, n_heads, HEAD_DIM, tm), lambda bi, i: (bi, 0, 0, i))
    out_shape = (
        jax.ShapeDtypeStruct((b, n_heads, HEAD_DIM, s), F32),
        jax.ShapeDtypeStruct((b, n_heads, HEAD_DIM, s), F32),
        jax.ShapeDtypeStruct((b, n_heads, s, LANES), BF16),
        jax.ShapeDtypeStruct((b, n_heads, nb, LANES, tm), BF16),
        jax.ShapeDtypeStruct((b, n_heads, nb, tm, LANES), BF16),
        jax.ShapeDtypeStruct((b, s, wb + wc), BF16),
        jax.ShapeDtypeStruct((b, nb, 1, wa), F32),
        jax.ShapeDtypeStruct((b, POOL_CARRY, wb), F32),
        jax.ShapeDtypeStruct((b, CONV_CARRY, wc), F32),
    )
    out_specs = (
        head_t, head_t,
        pl.BlockSpec((1, n_heads, tm, LANES), lambda bi, i: (bi, 0, i, 0)),
        pl.BlockSpec((1, n_heads, 1, LANES, tm), lambda bi, i: (bi, 0, i, 0, 0)),
        pl.BlockSpec((1, n_heads, 1, tm, LANES), lambda bi, i: (bi, 0, i, 0, 0)),
        seq(wb + wc), pl.BlockSpec((1, 1, 1, wa), lambda bi, i: (bi, i, 0, 0)),
        per_b(POOL_CARRY, wb), per_b(CONV_CARRY, wc),
    )
    return pl.pallas_call(
        functools.partial(_proj_kernel, wa=wa, wb=wb, wc=wc),
        grid=(b, nb),
        in_specs=[seq(d), _resident(nm.shape), _resident(w_in.shape), tab, tab,
                  _resident(wpool_bd.shape), _resident(pscale.shape), _resident(conv_w.shape)],
        out_specs=out_specs,
        out_shape=out_shape,
        scratch_shapes=[pltpu.VMEM((POOL_CARRY, wb), F32), pltpu.VMEM((CONV_CARRY, wc), F32)],
        compiler_params=_params(("parallel", "arbitrary")),
        name="proj_prompt",
    )(x, nm, w_in, cos, sin, wpool_bd, pscale, conv_w)


def _top_blocks_t(gate_t, n_valid):
    blk = lax.broadcasted_iota(jnp.int32, gate_t.shape, 0)
    blk_f = blk.astype(F32)
    g = jnp.where(blk < n_valid, gate_t, -jnp.inf)
    picked = jnp.zeros(gate_t.shape, jnp.bool_)
    for _ in range(MOBA_TOPK):
        m = jnp.max(g, axis=0, keepdims=True)
        first = jnp.min(jnp.where(g == m, blk_f, float(2 * LANES)), axis=0, keepdims=True)
        hit = (blk_f == first) & (m > -jnp.inf)
        picked = picked | hit
        g = jnp.where(hit, -jnp.inf, g)
    return jnp.where(picked, 0.0, NEG)


def _dot_nt(a, b):
    return lax.dot_general(a, b, (((1,), (1,)), ((), ())), preferred_element_type=F32)


def _logits(a, b):
    return jnp.dot(a, b, preferred_element_type=F32).astype(BF16)


ATTN_HEADS_PER_STEP = 4


def _attn_kernel(q_ref, kt_ref, va_ref, kmr_ref, o_ref):
    i = pl.program_id(2)
    tq = q_ref.shape[2]
    nb = kt_ref.shape[2]
    nbp = -(-nb // 8) * 8
    lane = lax.broadcasted_iota(jnp.int32, (tq, LANES), 1)
    row = lax.broadcasted_iota(jnp.int32, (tq, MOBA_BLOCK), 0)
    col = lax.broadcasted_iota(jnp.int32, (tq, MOBA_BLOCK), 1)
    n_h = q_ref.shape[1]
    q_sel, state = [], []
    for hh in range(n_h):
        q = q_ref[0, hh]
        km = kmr_ref[0, hh]
        km_hi = km.astype(BF16)
        km_lo = (km - km_hi.astype(F32)).astype(BF16)
        gate_t = _dot_nt(km_hi, q) + _dot_nt(km_lo, q)
        pen_t = _top_blocks_t(gate_t[HEAD_DIM:HEAD_DIM + nbp], i)
        pen_t = jnp.concatenate([jnp.zeros((HEAD_DIM, tq), F32), pen_t,
                                 jnp.zeros((LANES - HEAD_DIM - nbp, tq), F32)], axis=0)
        q_sel.append((q.astype(F32) + pen_t.T).astype(BF16))
        s = jnp.where(col <= row, _logits(q, kt_ref[0, hh, i]), NEG)
        m = jnp.max(s, axis=1, keepdims=True)
        state += [m.astype(F32), _dot(jnp.exp(s - m), va_ref[0, hh, i])]

    last_pair = nb // 2 - 1

    def logits(jj, hh):
        return [_logits(q_sel[hh], kt_ref[0, hh, 2 * jj]),
                _logits(q_sel[hh], kt_ref[0, hh, 2 * jj + 1])]

    def body(jj, carry):
        new = []
        nxt = jnp.minimum(jj + 1, last_pair)
        for hh in range(n_h):
            m, acc, s0, s1 = carry[4 * hh:4 * hh + 4]
            s_next = logits(nxt, hh)
            m_cur = jnp.max(jnp.maximum(s0, s1), axis=1, keepdims=True).astype(F32)
            m_new = jnp.maximum(m, m_cur)
            m_b = m_new.astype(BF16)
            acc = (acc * jnp.exp(m - m_new) + _dot(jnp.exp(s0 - m_b), va_ref[0, hh, 2 * jj])
                   + _dot(jnp.exp(s1 - m_b), va_ref[0, hh, 2 * jj + 1]))
            new += [m_new, acc] + s_next
        return tuple(new)

    init = ()
    for hh in range(n_h):
        init += tuple(state[2 * hh:2 * hh + 2] + logits(0, hh))
    state = lax.fori_loop(0, (i + 1) // 2, body, init)
    outs = [state[4 * hh + 1] / state[4 * hh + 1][:, HEAD_DIM:HEAD_DIM + 1] for hh in range(n_h)]
    for p in range(n_h // 2):
        o_ref[0, :, p * LANES:(p + 1) * LANES] = jnp.where(
            lane < HEAD_DIM, outs[2 * p], pltpu.roll(outs[2 * p + 1], HEAD_DIM, 1)).astype(BF16)


def _attn_prompt(q, kt, va, kmt):
    b, n_heads, s, _ = q.shape
    nb = kt.shape[2]
    tq = MOBA_BLOCK
    hg = ATTN_HEADS_PER_STEP
    assert nb % 2 == 0 and n_heads % hg == 0 and hg % 2 == 0
    return pl.pallas_call(
        _attn_kernel,
        grid=(b, n_heads // hg, nb),
        in_specs=[
            pl.BlockSpec((1, hg, tq, LANES), lambda bi, hp, i: (bi, hp, i, 0)),
            pl.BlockSpec((1, hg, nb, LANES, tq), lambda bi, hp, i: (bi, hp, 0, 0, 0)),
            pl.BlockSpec((1, hg, nb, tq, LANES), lambda bi, hp, i: (bi, hp, 0, 0, 0)),
            pl.BlockSpec((1, hg, LANES, LANES), lambda bi, hp, i: (bi, hp, 0, 0)),
        ],
        out_specs=pl.BlockSpec((1, tq, hg * HEAD_DIM), lambda bi, hp, i: (bi, i, hp)),
        out_shape=jax.ShapeDtypeStruct((b, s, n_heads * HEAD_DIM), BF16),
        compiler_params=_params(("parallel", "parallel", "arbitrary")),
        name="attn_prompt",
    )(q, kt, va, kmt)


def _gate_table(kmean, n_heads):
    b, nb = kmean.shape[:2]
    t = kmean.reshape(b, nb, n_heads, HEAD_DIM).transpose(0, 2, 1, 3)
    return jnp.pad(t, ((0, 0), (0, 0), (HEAD_DIM, LANES - HEAD_DIM - nb), (0, LANES - HEAD_DIM)))


def _sproj_kernel(x_ref, nm_ref, win_ref, cos_ref, sin_ref, wpool_ref, pscale_ref, convw_ref,
                  spool_ref, sconv_ref, q_ref, k_ref, v_ref, pc_ref, u_ref, z_ref,
                  *, wa, wb, wc, pos):
    h = _rms(x_ref[...], nm_ref[...]).astype(BF16)
    cos, sin = cos_ref[...], sin_ref[...]
    scale = HEAD_DIM ** -0.5
    for c in range(wa // LANES):
        sl = slice(c * LANES, (c + 1) * LANES)
        q_ref[:, sl] = _rope_chunk(_dot(h, win_ref[:, sl]), cos, sin) * scale
        k_ref[:, sl] = _rope_chunk(_dot(h, win_ref[:, wa + c * LANES: wa + (c + 1) * LANES]), cos, sin)
        v_ref[:, sl] = _dot(h, win_ref[:, 2 * wa + c * LANES: 2 * wa + (c + 1) * LANES])
    o = 3 * wa
    u = _dot(h, win_ref[:, o:o + wb])
    hc = _dot(h, win_ref[:, o + wb:o + wb + wc])
    bg = _dot(h, win_ref[:, o + wb + wc:o + wb + 2 * wc])
    cg = _dot(h, win_ref[:, o + wb + 2 * wc:o + wb + 3 * wc])

    sums, run = {}, u
    for r in range(1, max(POOL_WINDOWS)):
        run = run + spool_ref[:, POOL_STATE - r, :]
        if r + 1 in POOL_WINDOWS:
            sums[r + 1] = run
    posv = jnp.full((u.shape[0], 1), pos, jnp.int32)
    d = _pool_select(sums[2], sums[4], sums[8], sums[16], posv) - u
    p = _dot(d.astype(BF16), wpool_ref[...]) * pscale_ref[...]

    z = cg * hc
    conv = (z * convw_ref[2:3, :] + sconv_ref[:, 1, :] * convw_ref[1:2, :]
            + sconv_ref[:, 0, :] * convw_ref[0:1, :])
    u_ref[...] = u
    z_ref[...] = z
    pc_ref[:, :wb] = p.astype(BF16)
    pc_ref[:, wb:] = (bg * conv).astype(BF16)


def _proj_sample(x, nm, w_in, cos, sin, wpool_bd, pscale, conv_w, spool, sconv, *, n_heads, pos):
    m, _ = x.shape
    wa = n_heads * HEAD_DIM
    wb = wpool_bd.shape[0]
    wc = conv_w.shape[1]
    assert spool.shape[1] == POOL_STATE and sconv.shape[1] == CONV_K - 1
    f = lambda w, dt: jax.ShapeDtypeStruct((m, w), dt)
    return pl.pallas_call(
        functools.partial(_sproj_kernel, wa=wa, wb=wb, wc=wc, pos=pos),
        out_shape=(f(wa, F32), f(wa, F32), f(wa, F32), f(wb + wc, BF16), f(wb, F32), f(wc, F32)),
        compiler_params=pltpu.CompilerParams(vmem_limit_bytes=VMEM_LIMIT),
        name="proj_sample",
    )(x, nm, w_in, cos, sin, wpool_bd, pscale, conv_w, spool, sconv)


PAGES_PER_STEP = 16


def _sgate_kernel(pt_ref, *refs, pages_per_block, n_blocks):
    page_refs = refs[:PAGES_PER_STEP]
    q_ref, idx_ref, gate_ref = refs[PAGES_PER_STEP:]
    g = pl.program_id(1)
    blocks_per_step = PAGES_PER_STEP // pages_per_block
    n_heads = q_ref.shape[1]
    lane = lax.broadcasted_iota(jnp.int32, (n_heads, LANES), 1)

    @pl.when(g == 0)
    def _():
        gate_ref[...] = jnp.zeros_like(gate_ref)

    q = jnp.broadcast_to(q_ref[0], page_refs[0].shape[2:])
    gate = gate_ref[...]
    for r in range(blocks_per_step):
        tot = page_refs[r * pages_per_block][0, 0]
        for t in range(1, pages_per_block):
            tot = tot + page_refs[r * pages_per_block + t][0, 0]
        per_token = jnp.sum(tot * q, axis=1)
        val = jnp.sum(per_token, axis=1, keepdims=True) / float(MOBA_BLOCK)
        gate = jnp.where(lane == g * blocks_per_step + r, val, gate)
    gate_ref[...] = gate

    @pl.when(g == pl.num_programs(1) - 1)
    def _():
        lane_f = lane.astype(F32)
        gt = jnp.where(lane < n_blocks, gate, -jnp.inf)
        picks = jnp.zeros((n_heads, LANES), F32)
        for r in range(MOBA_TOPK):
            mx = jnp.max(gt, axis=1, keepdims=True)
            first = jnp.min(jnp.where(gt == mx, lane_f, float(LANES)), axis=1, keepdims=True)
            gt = jnp.where(lane_f == first, -jnp.inf, gt)
            picks = jnp.where(lane == r, first, picks)
        idx_ref[0] = picks.astype(jnp.int32)


def _sample_gates(cache_kt, page_table, q4, *, layer):
    _, n_pool, n_heads, _, page_size = cache_kt.shape
    b, n_pages = page_table.shape
    ppb = MOBA_BLOCK // page_size
    nbp = n_pages // ppb
    assert MOBA_BLOCK % page_size == 0 and PAGES_PER_STEP % ppb == 0
    assert n_pages % PAGES_PER_STEP == 0 and MOBA_TOPK <= nbp <= LANES
    page_spec = lambda r: pl.BlockSpec(
        (1, 1, n_heads, HEAD_DIM, page_size),
        lambda bi, g, pt: (layer, pt[bi, g * PAGES_PER_STEP + r], 0, 0, 0))
    return pl.pallas_call(
        functools.partial(_sgate_kernel, pages_per_block=ppb, n_blocks=nbp),
        grid_spec=pltpu.PrefetchScalarGridSpec(
            num_scalar_prefetch=1,
            grid=(b, n_pages // PAGES_PER_STEP),
            in_specs=[page_spec(r) for r in range(PAGES_PER_STEP)]
            + [pl.BlockSpec((1, n_heads, HEAD_DIM, 1), lambda bi, g, pt: (bi, 0, 0, 0))],
            out_specs=pl.BlockSpec((1, n_heads, LANES), lambda bi, g, pt: (bi, 0, 0)),
            scratch_shapes=[pltpu.VMEM((n_heads, LANES), F32)],
        ),
        out_shape=jax.ShapeDtypeStruct((b, n_heads, LANES), jnp.int32),
        compiler_params=_params(("parallel", "arbitrary")),
        name="sample_gates",
    )(page_table, *([cache_kt] * PAGES_PER_STEP), q4)


def _sattn_kernel(idx_ref, pt_ref, *refs, n_fetch):
    k_refs = refs[:2 * n_fetch]
    v_refs = refs[2 * n_fetch:4 * n_fetch]
    q_ref, kn_ref, vn_ref, o_ref = refs[4 * n_fetch:]
    for hh in range(2):
        q = q_ref[0, hh]
        s_self = jnp.sum(q * kn_ref[0, hh], axis=0, keepdims=True)
        scores = [jnp.sum(k_refs[hh * n_fetch + t][0, 0, 0] * q, axis=0, keepdims=True)
                  for t in range(n_fetch)]
        m = s_self
        for s in scores:
            m = jnp.maximum(m, jnp.max(s, axis=1, keepdims=True))
        p_self = jnp.exp(s_self - m)
        l = p_self
        acc = p_self * vn_ref[0, hh]
        for t, s in enumerate(scores):
            p = jnp.exp(s - m)
            l = l + jnp.sum(p, axis=1, keepdims=True)
            acc = acc + jnp.sum(v_refs[hh * n_fetch + t][0, 0, 0] * p, axis=1, keepdims=True)
        o_ref[0, hh] = acc / l


def _sample_attn(cache_kt, cache_vt, page_table, idx, q4, kn4, vn4, *, layer):
    _, n_pool, n_heads, _, page_size = cache_kt.shape
    b = page_table.shape[0]
    ppb = MOBA_BLOCK // page_size
    n_fetch = MOBA_TOPK * ppb

    def page_spec(hh, r, t):
        def index(bi, hp, idx_s, pt):
            head = 2 * hp + hh
            blk = idx_s[(bi * n_heads + head) * MOBA_TOPK + r]
            return (layer, pt[bi, blk * ppb + t], head, 0, 0)
        return pl.BlockSpec((1, 1, 1, HEAD_DIM, page_size), index)

    fetch = [page_spec(hh, r, t) for hh in range(2) for r in range(MOBA_TOPK) for t in range(ppb)]
    vec = pl.BlockSpec((1, 2, HEAD_DIM, 1), lambda bi, hp, idx_s, pt: (bi, hp, 0, 0))
    return pl.pallas_call(
        functools.partial(_sattn_kernel, n_fetch=n_fetch),
        grid_spec=pltpu.PrefetchScalarGridSpec(
            num_scalar_prefetch=2,
            grid=(b, n_heads // 2),
            in_specs=fetch + fetch + [vec, vec, vec],
            out_specs=vec,
        ),
        out_shape=jax.ShapeDtypeStruct((b, n_heads, HEAD_DIM, 1), F32),
        compiler_params=_params(("parallel", "arbitrary")),
        name="sample_attn",
    )(idx.reshape(-1), page_table, *([cache_kt] * len(fetch)), *([cache_vt] * len(fetch)),
      q4, kn4, vn4)


def kernel(x_prompt, x_sample, cache_k, cache_v, page_table, state_pool, state_conv, norm_ffn1, ffn1_gate, ffn1_up, ffn1_down, norm_mix, w_in, w_pool, pool_scale, conv_w, w_o, norm_ffn2, ffn2_gate, ffn2_up, ffn2_down, norm_final):
    b, s, d = x_prompt.shape
    db, t_new, _ = x_sample.shape
    depth, n_pool, page_size, n_heads, head_dim = cache_k.shape
    assert head_dim == HEAD_DIM and t_new == 1 and n_heads % 2 == 0
    past_len = page_table.shape[1] * page_size
    assert past_len % MOBA_BLOCK == 0 and past_len + 1 >= max(POOL_WINDOWS)
    wa = n_heads * HEAD_DIM
    tm_prompt = 512 if (b * s) % 512 == 0 else MOBA_BLOCK

    cos_p, sin_p = _rope_tables(jnp.arange(s, dtype=jnp.int32))
    cos_s, sin_s = _rope_tables(past_len + jnp.arange(1, dtype=jnp.int32))
    row = lambda a: a.reshape(1, -1)
    bf = lambda a: a.astype(BF16)

    cache_kt = cache_k.transpose(0, 1, 3, 4, 2)
    cache_vt = cache_v.transpose(0, 1, 3, 4, 2)

    xp = x_prompt.reshape(b * s, d)
    xs = x_sample.reshape(db, d)
    outs = {n: [] for n in ("kp", "vp", "ks", "vs", "pp", "ps", "cp", "cs")}
    for l in range(depth):
        n_groups = w_pool.shape[1]
        wpool_bd = bf(jax.scipy.linalg.block_diag(*[w_pool[l, g] for g in range(n_groups)]))
        f1 = (row(norm_ffn1[l]), bf(ffn1_gate[l]), bf(ffn1_up[l]), bf(ffn1_down[l]))
        f2 = (row(norm_ffn2[l]), bf(ffn2_gate[l]), bf(ffn2_up[l]), bf(ffn2_down[l]))
        mixer_w = (row(norm_mix[l]), bf(w_in[l]))
        mixer_tail = (wpool_bd, row(pool_scale[l]), conv_w[l])
        wo = bf(w_o[l])

        xp = _ffn(xp, *f1, tm=tm_prompt)
        kp, vp, q, kt, va, pc, kmean, ptail, ctail = _proj_prompt(
            xp.reshape(b, s, d), *mixer_w, cos_p, sin_p, *mixer_tail, n_heads=n_heads)
        a = _attn_prompt(q, kt, va, _gate_table(kmean, n_heads))
        last = l == depth - 1
        xp = _ffn(xp, *f2, mix=(a.reshape(b * s, wa), pc.reshape(b * s, -1), wo),
                  final=row(norm_final) if last else None, tm=tm_prompt)
        outs["kp"].append(kp.transpose(0, 3, 1, 2))
        outs["vp"].append(vp.transpose(0, 3, 1, 2))
        outs["pp"].append(ptail[:, POOL_CARRY - POOL_STATE:])
        outs["cp"].append(ctail[:, CONV_CARRY - (CONV_K - 1):])

        xs = _ffn(xs, *f1, tm=db)
        qs, ksn, vsn, pcs, us, zs = _proj_sample(
            xs, *mixer_w, cos_s, sin_s, *mixer_tail, state_pool[l], state_conv[l],
            n_heads=n_heads, pos=past_len)
        col = lambda a: a.reshape(db, n_heads, HEAD_DIM, 1)
        idx = _sample_gates(cache_kt, page_table, col(qs), layer=l)
        a_s = _sample_attn(cache_kt, cache_vt, page_table, idx[:, :, :MOBA_TOPK],
                           col(qs), col(ksn), col(vsn), layer=l)
        xs = _ffn(xs, *f2, mix=(bf(a_s.reshape(db, wa)), pcs, wo),
                  final=row(norm_final) if last else None, tm=db)
        outs["ks"].append(ksn.reshape(db, 1, n_heads, HEAD_DIM))
        outs["vs"].append(vsn.reshape(db, 1, n_heads, HEAD_DIM))
        outs["ps"].append(jnp.concatenate([state_pool[l][:, 1:], us[:, None]], axis=1))
        outs["cs"].append(jnp.concatenate([state_conv[l][:, 1:], zs[:, None]], axis=1))

    st = lambda n: jnp.stack(outs[n])
    return (xp.reshape(b, s, d), xs.reshape(db, 1, d), st("kp"), st("vp"), st("ks"), st("vs"),
            st("pp"), st("ps"), st("cp"), st("cs"))
```

```python
import functools

import jax
import jax.numpy as jnp
from jax import lax
from jax.experimental import pallas as pl
from jax.experimental.pallas import tpu as pltpu

F32 = jnp.float32
BF16 = jnp.bfloat16

HEAD_DIM = 64
LANES = 128
MXU_DIM = 256
MOBA_BLOCK = 256
MOBA_TOPK = 3
POOL_WINDOWS = (2, 4, 8, 16)
POOL_STATE = max(POOL_WINDOWS) - 1
POOL_CARRY = 16
CONV_K = 3
CONV_CARRY = 8
ROPE_THETA = 10000.0
RMS_EPS = 1e-6
NEG = -1e30
VMEM_LIMIT = 56 * 1024 * 1024


def _dot(a, b):
    return jnp.dot(a, b, preferred_element_type=F32)


def _rms(x, g):
    return x * lax.rsqrt(jnp.mean(x * x, axis=-1, keepdims=True) + RMS_EPS) * g


def _resident(shape):
    return pl.BlockSpec(shape, lambda *_: (0,) * len(shape), pipeline_mode=pl.Buffered(1))


def _params(sem):
    return pltpu.CompilerParams(dimension_semantics=sem, vmem_limit_bytes=VMEM_LIMIT)


def _ffn_kernel(*refs, premix, final_norm, ff_chunks):
    it = iter(refs)
    x_ref = next(it)
    if premix:
        a_ref, pc_ref, wo_ref = next(it), next(it), next(it)
    n_ref, wg_ref, wu_ref, wd_ref = next(it), next(it), next(it), next(it)
    if final_norm:
        nf_ref = next(it)
    o_ref = next(it)

    x = x_ref[...]
    if premix:
        wa = a_ref.shape[-1]
        x = x + _dot(a_ref[...], wo_ref[:wa, :]) + _dot(pc_ref[...], wo_ref[wa:, :])
    h = _rms(x, n_ref[...]).astype(BF16)
    y = jnp.zeros_like(x)
    for lo, hi in ff_chunks:
        g = _dot(h, wg_ref[:, lo:hi])
        u = _dot(h, wu_ref[:, lo:hi])
        act = (g * jax.nn.sigmoid(g) * u).astype(BF16)
        y = y + _dot(act, wd_ref[lo:hi, :])
    x = x + 0.5 * y
    if final_norm:
        x = _rms(x, nf_ref[...])
    o_ref[...] = x


def _ffn(x, norm, wg, wu, wd, *, mix=None, final=None, tm):
    m, d = x.shape
    ff = wg.shape[1]
    half = -(-(ff // 2) // MXU_DIM) * MXU_DIM
    ff_chunks = ((0, half), (half, ff))
    row = lambda w: pl.BlockSpec((tm, w), lambda i: (i, 0))
    args, specs = [x], [row(d)]
    if mix is not None:
        a, pc, wo = mix
        args += [a, pc, wo]
        specs += [row(a.shape[1]), row(pc.shape[1]), _resident(wo.shape)]
    args += [norm, wg, wu, wd]
    specs += [_resident(norm.shape), _resident(wg.shape), _resident(wu.shape), _resident(wd.shape)]
    if final is not None:
        args.append(final)
        specs.append(_resident(final.shape))
    return pl.pallas_call(
        functools.partial(_ffn_kernel, premix=mix is not None, final_norm=final is not None,
                          ff_chunks=ff_chunks),
        grid=(m // tm,),
        in_specs=specs,
        out_specs=row(d),
        out_shape=jax.ShapeDtypeStruct((m, d), F32),
        compiler_params=_params(("parallel",)),
        name="ffn",
    )(*args)


def _rope_chunk(x, cos, sin_signed):
    lane = lax.broadcasted_iota(jnp.int32, x.shape, 1)
    first_half = (lane % HEAD_DIM) < (HEAD_DIM // 2)
    partner = jnp.where(first_half, pltpu.roll(x, LANES - HEAD_DIM // 2, 1),
                        pltpu.roll(x, HEAD_DIM // 2, 1))
    return x * cos + partner * sin_signed


def _pool_select(s2, s4, s8, s16, pos):
    lane = lax.broadcasted_iota(jnp.int32, s2.shape, 1)
    group = s2.shape[1] // len(POOL_WINDOWS)
    posf = (pos + 1).astype(F32)
    mean = lambda s, w: s / jnp.minimum(posf, float(w))
    return jnp.where(lane < group, mean(s2, 2),
                     jnp.where(lane < 2 * group, mean(s4, 4),
                               jnp.where(lane < 3 * group, mean(s8, 8), mean(s16, 16))))


def _rope_tables(pos):
    half = HEAD_DIM // 2
    inv = jnp.power(ROPE_THETA, -jnp.arange(half, dtype=F32) / half)
    ang = pos.astype(F32)[:, None] * inv[None, :]
    cos, sin = jnp.cos(ang), jnp.sin(ang)
    reps = LANES // HEAD_DIM
    return (jnp.tile(jnp.concatenate([cos, cos], axis=1), (1, reps)),
            jnp.tile(jnp.concatenate([-sin, sin], axis=1), (1, reps)))


def _proj_kernel(x_ref, nm_ref, win_ref, cos_ref, sin_ref, wpool_ref, pscale_ref, convw_ref,
                 k_ref, v_ref, q_ref, kt_ref, va_ref, pc_ref, km_ref, ptail_ref, ctail_ref,
                 prevu_ref, prevz_ref, *, wa, wb, wc):
    i = pl.program_id(1)
    tm = x_ref.shape[1]
    n_chunks = wa // LANES

    @pl.when(i == 0)
    def _():
        prevu_ref[...] = jnp.zeros_like(prevu_ref)
        prevz_ref[...] = jnp.zeros_like(prevz_ref)

    h = _rms(x_ref[0], nm_ref[...]).astype(BF16)
    cos, sin = cos_ref[...], sin_ref[...]
    lane = lax.broadcasted_iota(jnp.int32, (tm, LANES), 1)
    low = lane < HEAD_DIM
    ones_col = (lane == HEAD_DIM).astype(F32)
    block_onehot_t = (lax.broadcasted_iota(jnp.int32, (HEAD_DIM, tm), 0) == i).astype(BF16)
    scale = HEAD_DIM ** -0.5

    for c in range(n_chunks):
        sl = slice(c * LANES, (c + 1) * LANES)
        q = _rope_chunk(_dot(h, win_ref[:, sl]), cos, sin) * scale
        k = _rope_chunk(_dot(h, win_ref[:, wa + c * LANES: wa + (c + 1) * LANES]), cos, sin)
        v = _dot(h, win_ref[:, 2 * wa + c * LANES: 2 * wa + (c + 1) * LANES])
        km_ref[0, 0, :, sl] = jnp.mean(k, axis=0, keepdims=True)
        k_t, v_t = k.T, v.T
        for hh in range(2):
            head = 2 * c + hh
            rows = slice(hh * HEAD_DIM, (hh + 1) * HEAD_DIM)
            k_ref[0, head] = k_t[rows]
            v_ref[0, head] = v_t[rows]
            kt_ref[0, head, 0] = jnp.concatenate([k_t[rows].astype(BF16), block_onehot_t], axis=0)
            qh, vh = (q, v) if hh == 0 else (pltpu.roll(q, HEAD_DIM, 1), pltpu.roll(v, HEAD_DIM, 1))
            q_ref[0, head] = jnp.where(low, qh, 0.0).astype(BF16)
            va_ref[0, head, 0] = jnp.where(low, vh, ones_col).astype(BF16)

    o = 3 * wa
    u = _dot(h, win_ref[:, o:o + wb])
    hc = _dot(h, win_ref[:, o + wb:o + wb + wc])
    bg = _dot(h, win_ref[:, o + wb + wc:o + wb + 2 * wc])
    cg = _dot(h, win_ref[:, o + wb + 2 * wc:o + wb + 3 * wc])

    pos = i * tm + lax.broadcasted_iota(jnp.int32, (tm, 1), 0)
    e = jnp.concatenate([prevu_ref[...], u], axis=0)
    s2 = e + pltpu.roll(e, 1, 0)
    s4 = s2 + pltpu.roll(s2, 2, 0)
    s8 = s4 + pltpu.roll(s4, 4, 0)
    s16 = s8 + pltpu.roll(s8, 8, 0)
    t = slice(POOL_CARRY, POOL_CARRY + tm)
    d = _pool_select(s2[t], s4[t], s8[t], s16[t], pos) - u
    p = _dot(d.astype(BF16), wpool_ref[...]) * pscale_ref[...]
    prevu_ref[...] = u[tm - POOL_CARRY:]
    ptail_ref[0] = u[tm - POOL_CARRY:]

    z = cg * hc
    ze = jnp.concatenate([prevz_ref[...], z], axis=0)
    tz = slice(CONV_CARRY, CONV_CARRY + tm)
    conv = (z * convw_ref[2:3, :] + pltpu.roll(ze, 1, 0)[tz] * convw_ref[1:2, :]
            + pltpu.roll(ze, 2, 0)[tz] * convw_ref[0:1, :])
    prevz_ref[...] = z[tm - CONV_CARRY:]
    ctail_ref[0] = z[tm - CONV_CARRY:]
    pc_ref[0, :, :wb] = p.astype(BF16)
    pc_ref[0, :, wb:] = (bg * conv).astype(BF16)


def _proj_prompt(x, nm, w_in, cos, sin, wpool_bd, pscale, conv_w, *, n_heads):
    b, s, d = x.shape
    tm = MOBA_BLOCK
    nb = s // tm
    wa = n_heads * HEAD_DIM
    wb = wpool_bd.shape[0]
    wc = conv_w.shape[1]
    assert s % tm == 0 and nb <= LANES - HEAD_DIM and wa % LANES == 0
    assert w_in.shape[1] == 3 * wa + wb + 3 * wc
    seq = lambda w: pl.BlockSpec((1, tm, w), lambda bi, i: (bi, i, 0))
    tab = pl.BlockSpec((tm, LANES), lambda bi, i: (i, 0))
    per_b = lambda r, w: pl.BlockSpec((1, r, w), lambda bi, i: (bi, 0, 0))
    head_t = pl.BlockSpec((1, n_heads, HEAD_DIM, tm), lambda bi, i: (bi, 0, 0, i))
    out_shape = (
        jax.ShapeDtypeStruct((b, n_heads, HEAD_DIM, s), F32),
        jax.ShapeDtypeStruct((b, n_heads, HEAD_DIM, s), F32),
        jax.ShapeDtypeStruct((b, n_heads, s, LANES), BF16),
        jax.ShapeDtypeStruct((b, n_heads, nb, LANES, tm), BF16),
        jax.ShapeDtypeStruct((b, n_heads, nb, tm, LANES), BF16),
        jax.ShapeDtypeStruct((b, s, wb + wc), BF16),
        jax.ShapeDtypeStruct((b, nb, 1, wa), F32),
        jax.ShapeDtypeStruct((b, POOL_CARRY, wb), F32),
        jax.ShapeDtypeStruct((b, CONV_CARRY, wc), F32),
    )
    out_specs = (
        head_t, head_t,
        pl.BlockSpec((1, n_heads, tm, LANES), lambda bi, i: (bi, 0, i, 0)),
        pl.BlockSpec((1, n_heads, 1, LANES, tm), lambda bi, i: (bi, 0, i, 0, 0)),
        pl.BlockSpec((1, n_heads, 1, tm, LANES), lambda bi, i: (bi, 0, i, 0, 0)),
        seq(wb + wc), pl.BlockSpec((1, 1, 1, wa), lambda bi, i: (bi, i, 0, 0)),
        per_b(POOL_CARRY, wb), per_b(CONV_CARRY, wc),
    )
    return pl.pallas_call(
        functools.partial(_proj_kernel, wa=wa, wb=wb, wc=wc),
        grid=(b, nb),
        in_specs=[seq(d), _resident(nm.shape), _resident(w_in.shape), tab, tab,
                  _resident(wpool_bd.shape), _resident(pscale.shape), _resident(conv_w.shape)],
        out_specs=out_specs,
        out_shape=out_shape,
        scratch_shapes=[pltpu.VMEM((POOL_CARRY, wb), F32), pltpu.VMEM((CONV_CARRY, wc), F32)],
        compiler_params=_params(("parallel", "arbitrary")),
        name="proj_prompt",
    )(x, nm, w_in, cos, sin, wpool_bd, pscale, conv_w)


def _top_blocks_t(gate_t, n_valid):
    blk = lax.broadcasted_iota(jnp.int32, gate_t.shape, 0)
    blk_f = blk.astype(F32)
    g = jnp.where(blk < n_valid, gate_t, -jnp.inf)
    picked = jnp.zeros(gate_t.shape, jnp.bool_)
    for _ in range(MOBA_TOPK):
        m = jnp.max(g, axis=0, keepdims=True)
        first = jnp.min(jnp.where(g == m, blk_f, float(2 * LANES)), axis=0, keepdims=True)
        hit = (blk_f == first) & (m > -jnp.inf)
        picked = picked | hit
        g = jnp.where(hit, -jnp.inf, g)
    return jnp.where(picked, 0.0, NEG)


def _dot_nt(a, b):
    return lax.dot_general(a, b, (((1,), (1,)), ((), ())), preferred_element_type=F32)


def _logits(a, b):
    return jnp.dot(a, b, preferred_element_type=F32).astype(BF16)


ATTN_BLOCKS_PER_TRIP = 2
ATTN_HEADS_PER_STEP = 8


def _attn_kernel(q_ref, kt_ref, va_ref, kmr_ref, o_ref, qs_ref, m_ref, acc_ref, s_ref):
    i = pl.program_id(2)
    tq = q_ref.shape[2]
    nb = kt_ref.shape[2]
    nbp = -(-nb // 8) * 8
    lane = lax.broadcasted_iota(jnp.int32, (tq, LANES), 1)
    row = lax.broadcasted_iota(jnp.int32, (tq, MOBA_BLOCK), 0)
    col = lax.broadcasted_iota(jnp.int32, (tq, MOBA_BLOCK), 1)
    n_h = q_ref.shape[1]
    g = ATTN_BLOCKS_PER_TRIP
    last_group = nb // g - 1

    def store_logits(slot, jj, hh):
        for t in range(g):
            s_ref[slot, hh, t] = _logits(qs_ref[hh], kt_ref[0, hh, g * jj + t])

    for hh in range(n_h):
        q = q_ref[0, hh]
        km = kmr_ref[0, hh]
        km_hi = km.astype(BF16)
        km_lo = (km - km_hi.astype(F32)).astype(BF16)
        gate_t = _dot_nt(km_hi, q) + _dot_nt(km_lo, q)
        pen_t = _top_blocks_t(gate_t[HEAD_DIM:HEAD_DIM + nbp], i)
        pen_t = jnp.concatenate([jnp.zeros((HEAD_DIM, tq), F32), pen_t,
                                 jnp.zeros((LANES - HEAD_DIM - nbp, tq), F32)], axis=0)
        qs_ref[hh] = (q.astype(F32) + pen_t.T).astype(BF16)
        s = jnp.where(col <= row, _logits(q, kt_ref[0, hh, i]), NEG)
        m = jnp.max(s, axis=1, keepdims=True)
        m_ref[hh] = m.astype(F32)
        acc_ref[hh] = _dot(jnp.exp(s - m), va_ref[0, hh, i])
        store_logits(0, 0, hh)

    @pl.loop(0, (i + g - 1) // g)
    def _(jj):
        slot = jj % 2
        for hh in range(n_h):
            s = [s_ref[slot, hh, t] for t in range(g)]
            m = m_ref[hh]
            s_max = functools.reduce(jnp.maximum, s)
            m_new = jnp.maximum(m, jnp.max(s_max, axis=1, keepdims=True).astype(F32))
            m_b = m_new.astype(BF16)
            p = jnp.concatenate([jnp.exp(s[t] - m_b) for t in range(g)], axis=1)
            v = va_ref[0, hh, pl.ds(g * jj, g)].reshape(g * tq, LANES)
            acc_ref[hh] = acc_ref[hh] * jnp.exp(m - m_new) + _dot(p, v)
            m_ref[hh] = m_new
        nxt = jnp.minimum(jj + 1, last_group)
        for hh in range(n_h):
            store_logits(1 - slot, nxt, hh)

    outs = [acc_ref[hh] / acc_ref[hh][:, HEAD_DIM:HEAD_DIM + 1] for hh in range(n_h)]
    for p in range(n_h // 2):
        o_ref[0, :, p * LANES:(p + 1) * LANES] = jnp.where(
            lane < HEAD_DIM, outs[2 * p], pltpu.roll(outs[2 * p + 1], HEAD_DIM, 1)).astype(BF16)


def _attn_prompt(q, kt, va, kmt):
    b, n_heads, s, _ = q.shape
    nb = kt.shape[2]
    tq = MOBA_BLOCK
    hg = ATTN_HEADS_PER_STEP
    assert nb % ATTN_BLOCKS_PER_TRIP == 0 and n_heads % hg == 0 and hg % 2 == 0
    return pl.pallas_call(
        _attn_kernel,
        grid=(b, n_heads // hg, nb),
        in_specs=[
            pl.BlockSpec((1, hg, tq, LANES), lambda bi, hp, i: (bi, hp, i, 0)),
            pl.BlockSpec((1, hg, nb, LANES, tq), lambda bi, hp, i: (bi, hp, 0, 0, 0),
                         pipeline_mode=pl.Buffered(1)),
            pl.BlockSpec((1, hg, nb, tq, LANES), lambda bi, hp, i: (bi, hp, 0, 0, 0),
                         pipeline_mode=pl.Buffered(1)),
            pl.BlockSpec((1, hg, LANES, LANES), lambda bi, hp, i: (bi, hp, 0, 0)),
        ],
        out_specs=pl.BlockSpec((1, tq, hg * HEAD_DIM), lambda bi, hp, i: (bi, i, hp)),
        out_shape=jax.ShapeDtypeStruct((b, s, n_heads * HEAD_DIM), BF16),
        scratch_shapes=[
            pltpu.VMEM((hg, tq, LANES), BF16),
            pltpu.VMEM((hg, tq, 1), F32),
            pltpu.VMEM((hg, tq, LANES), F32),
            pltpu.VMEM((2, hg, ATTN_BLOCKS_PER_TRIP, tq, MOBA_BLOCK), BF16),
        ],
        compiler_params=_params(("parallel", "parallel", "arbitrary")),
        name="attn_prompt",
    )(q, kt, va, kmt)


def _gate_table(kmean, n_heads):
    b, nb = kmean.shape[:2]
    t = kmean.reshape(b, nb, n_heads, HEAD_DIM).transpose(0, 2, 1, 3)
    return jnp.pad(t, ((0, 0), (0, 0), (HEAD_DIM, LANES - HEAD_DIM - nb), (0, LANES - HEAD_DIM)))


def _sproj_kernel(x_ref, nm_ref, win_ref, cos_ref, sin_ref, wpool_ref, pscale_ref, convw_ref,
                  spool_ref, sconv_ref, q_ref, k_ref, v_ref, pc_ref, u_ref, z_ref,
                  *, wa, wb, wc, pos):
    h = _rms(x_ref[...], nm_ref[...]).astype(BF16)
    cos, sin = cos_ref[...], sin_ref[...]
    scale = HEAD_DIM ** -0.5
    for c in range(wa // LANES):
        sl = slice(c * LANES, (c + 1) * LANES)
        q_ref[:, sl] = _rope_chunk(_dot(h, win_ref[:, sl]), cos, sin) * scale
        k_ref[:, sl] = _rope_chunk(_dot(h, win_ref[:, wa + c * LANES: wa + (c + 1) * LANES]), cos, sin)
        v_ref[:, sl] = _dot(h, win_ref[:, 2 * wa + c * LANES: 2 * wa + (c + 1) * LANES])
    o = 3 * wa
    u = _dot(h, win_ref[:, o:o + wb])
    hc = _dot(h, win_ref[:, o + wb:o + wb + wc])
    bg = _dot(h, win_ref[:, o + wb + wc:o + wb + 2 * wc])
    cg = _dot(h, win_ref[:, o + wb + 2 * wc:o + wb + 3 * wc])

    sums, run = {}, u
    for r in range(1, max(POOL_WINDOWS)):
        run = run + spool_ref[:, POOL_STATE - r, :]
        if r + 1 in POOL_WINDOWS:
            sums[r + 1] = run
    posv = jnp.full((u.shape[0], 1), pos, jnp.int32)
    d = _pool_select(sums[2], sums[4], sums[8], sums[16], posv) - u
    p = _dot(d.astype(BF16), wpool_ref[...]) * pscale_ref[...]

    z = cg * hc
    conv = (z * convw_ref[2:3, :] + sconv_ref[:, 1, :] * convw_ref[1:2, :]
            + sconv_ref[:, 0, :] * convw_ref[0:1, :])
    u_ref[...] = u
    z_ref[...] = z
    pc_ref[:, :wb] = p.astype(BF16)
    pc_ref[:, wb:] = (bg * conv).astype(BF16)


def _proj_sample(x, nm, w_in, cos, sin, wpool_bd, pscale, conv_w, spool, sconv, *, n_heads, pos):
    m, _ = x.shape
    wa = n_heads * HEAD_DIM
    wb = wpool_bd.shape[0]
    wc = conv_w.shape[1]
    assert spool.shape[1] == POOL_STATE and sconv.shape[1] == CONV_K - 1
    f = lambda w, dt: jax.ShapeDtypeStruct((m, w), dt)
    return pl.pallas_call(
        functools.partial(_sproj_kernel, wa=wa, wb=wb, wc=wc, pos=pos),
        out_shape=(f(wa, F32), f(wa, F32), f(wa, F32), f(wb + wc, BF16), f(wb, F32), f(wc, F32)),
        compiler_params=pltpu.CompilerParams(vmem_limit_bytes=VMEM_LIMIT),
        name="proj_sample",
    )(x, nm, w_in, cos, sin, wpool_bd, pscale, conv_w, spool, sconv)


PAGES_PER_STEP = 16


def _sgate_kernel(pt_ref, *refs, pages_per_block, n_blocks):
    page_refs = refs[:PAGES_PER_STEP]
    q_ref, idx_ref, gate_ref = refs[PAGES_PER_STEP:]
    g = pl.program_id(1)
    blocks_per_step = PAGES_PER_STEP // pages_per_block
    n_heads = q_ref.shape[1]
    lane = lax.broadcasted_iota(jnp.int32, (n_heads, LANES), 1)

    @pl.when(g == 0)
    def _():
        gate_ref[...] = jnp.zeros_like(gate_ref)

    q = jnp.broadcast_to(q_ref[0], page_refs[0].shape[2:])
    gate = gate_ref[...]
    for r in range(blocks_per_step):
        tot = page_refs[r * pages_per_block][0, 0]
        for t in range(1, pages_per_block):
            tot = tot + page_refs[r * pages_per_block + t][0, 0]
        per_token = jnp.sum(tot * q, axis=1)
        val = jnp.sum(per_token, axis=1, keepdims=True) / float(MOBA_BLOCK)
        gate = jnp.where(lane == g * blocks_per_step + r, val, gate)
    gate_ref[...] = gate

    @pl.when(g == pl.num_programs(1) - 1)
    def _():
        lane_f = lane.astype(F32)
        gt = jnp.where(lane < n_blocks, gate, -jnp.inf)
        picks = jnp.zeros((n_heads, LANES), F32)
        for r in range(MOBA_TOPK):
            mx = jnp.max(gt, axis=1, keepdims=True)
            first = jnp.min(jnp.where(gt == mx, lane_f, float(LANES)), axis=1, keepdims=True)
            gt = jnp.where(lane_f == first, -jnp.inf, gt)
            picks = jnp.where(lane == r, first, picks)
        idx_ref[0] = picks.astype(jnp.int32)


def _sample_gates(cache_kt, page_table, q4, *, layer):
    _, n_pool, n_heads, _, page_size = cache_kt.shape
    b, n_pages = page_table.shape
    ppb = MOBA_BLOCK // page_size
    nbp = n_pages // ppb
    assert MOBA_BLOCK % page_size == 0 and PAGES_PER_STEP % ppb == 0
    assert n_pages % PAGES_PER_STEP == 0 and MOBA_TOPK <= nbp <= LANES
    page_spec = lambda r: pl.BlockSpec(
        (1, 1, n_heads, HEAD_DIM, page_size),
        lambda bi, g, pt: (layer, pt[bi, g * PAGES_PER_STEP + r], 0, 0, 0))
    return pl.pallas_call(
        functools.partial(_sgate_kernel, pages_per_block=ppb, n_blocks=nbp),
        grid_spec=pltpu.PrefetchScalarGridSpec(
            num_scalar_prefetch=1,
            grid=(b, n_pages // PAGES_PER_STEP),
            in_specs=[page_spec(r) for r in range(PAGES_PER_STEP)]
            + [pl.BlockSpec((1, n_heads, HEAD_DIM, 1), lambda bi, g, pt: (bi, 0, 0, 0))],
            out_specs=pl.BlockSpec((1, n_heads, LANES), lambda bi, g, pt: (bi, 0, 0)),
            scratch_shapes=[pltpu.VMEM((n_heads, LANES), F32)],
        ),
        out_shape=jax.ShapeDtypeStruct((b, n_heads, LANES), jnp.int32),
        compiler_params=_params(("parallel", "arbitrary")),
        name="sample_gates",
    )(page_table, *([cache_kt] * PAGES_PER_STEP), q4)


def _sattn_kernel(idx_ref, pt_ref, *refs, n_fetch):
    k_refs = refs[:2 * n_fetch]
    v_refs = refs[2 * n_fetch:4 * n_fetch]
    q_ref, kn_ref, vn_ref, o_ref = refs[4 * n_fetch:]
    for hh in range(2):
        q = q_ref[0, hh]
        s_self = jnp.sum(q * kn_ref[0, hh], axis=0, keepdims=True)
        scores = [jnp.sum(k_refs[hh * n_fetch + t][0, 0, 0] * q, axis=0, keepdims=True)
                  for t in range(n_fetch)]
        m = s_self
        for s in scores:
            m = jnp.maximum(m, jnp.max(s, axis=1, keepdims=True))
        p_self = jnp.exp(s_self - m)
        l = p_self
        acc = p_self * vn_ref[0, hh]
        for t, s in enumerate(scores):
            p = jnp.exp(s - m)
            l = l + jnp.sum(p, axis=1, keepdims=True)
            acc = acc + jnp.sum(v_refs[hh * n_fetch + t][0, 0, 0] * p, axis=1, keepdims=True)
        o_ref[0, hh] = acc / l


def _sample_attn(cache_kt, cache_vt, page_table, idx, q4, kn4, vn4, *, layer):
    _, n_pool, n_heads, _, page_size = cache_kt.shape
    b = page_table.shape[0]
    ppb = MOBA_BLOCK // page_size
    n_fetch = MOBA_TOPK * ppb

    def page_spec(hh, r, t):
        def index(bi, hp, idx_s, pt):
            head = 2 * hp + hh
            blk = idx_s[(bi * n_heads + head) * MOBA_TOPK + r]
            return (layer, pt[bi, blk * ppb + t], head, 0, 0)
        return pl.BlockSpec((1, 1, 1, HEAD_DIM, page_size), index)

    fetch = [page_spec(hh, r, t) for hh in range(2) for r in range(MOBA_TOPK) for t in range(ppb)]
    vec = pl.BlockSpec((1, 2, HEAD_DIM, 1), lambda bi, hp, idx_s, pt: (bi, hp, 0, 0))
    return pl.pallas_call(
        functools.partial(_sattn_kernel, n_fetch=n_fetch),
        grid_spec=pltpu.PrefetchScalarGridSpec(
            num_scalar_prefetch=2,
            grid=(b, n_heads // 2),
            in_specs=fetch + fetch + [vec, vec, vec],
            out_specs=vec,
        ),
        out_shape=jax.ShapeDtypeStruct((b, n_heads, HEAD_DIM, 1), F32),
        compiler_params=_params(("parallel", "arbitrary")),
        name="sample_attn",
    )(idx.reshape(-1), page_table, *([cache_kt] * len(fetch)), *([cache_vt] * len(fetch)),
      q4, kn4, vn4)


def kernel(x_prompt, x_sample, cache_k, cache_v, page_table, state_pool, state_conv, norm_ffn1, ffn1_gate, ffn1_up, ffn1_down, norm_mix, w_in, w_pool, pool_scale, conv_w, w_o, norm_ffn2, ffn2_gate, ffn2_up, ffn2_down, norm_final):
    b, s, d = x_prompt.shape
    db, t_new, _ = x_sample.shape
    depth, n_pool, page_size, n_heads, head_dim = cache_k.shape
    assert head_dim == HEAD_DIM and t_new == 1 and n_heads % 2 == 0
    past_len = page_table.shape[1] * page_size
    assert past_len % MOBA_BLOCK == 0 and past_len + 1 >= max(POOL_WINDOWS)
    wa = n_heads * HEAD_DIM
    tm_prompt = 512 if (b * s) % 512 == 0 else MOBA_BLOCK

    cos_p, sin_p = _rope_tables(jnp.arange(s, dtype=jnp.int32))
    cos_s, sin_s = _rope_tables(past_len + jnp.arange(1, dtype=jnp.int32))
    row = lambda a: a.reshape(1, -1)
    bf = lambda a: a.astype(BF16)

    cache_kt = cache_k.transpose(0, 1, 3, 4, 2)
    cache_vt = cache_v.transpose(0, 1, 3, 4, 2)

    xp = x_prompt.reshape(b * s, d)
    xs = x_sample.reshape(db, d)
    outs = {n: [] for n in ("kp", "vp", "ks", "vs", "pp", "ps", "cp", "cs")}
    for l in range(depth):
        n_groups = w_pool.shape[1]
        wpool_bd = bf(jax.scipy.linalg.block_diag(*[w_pool[l, g] for g in range(n_groups)]))
        f1 = (row(norm_ffn1[l]), bf(ffn1_gate[l]), bf(ffn1_up[l]), bf(ffn1_down[l]))
        f2 = (row(norm_ffn2[l]), bf(ffn2_gate[l]), bf(ffn2_up[l]), bf(ffn2_down[l]))
        mixer_w = (row(norm_mix[l]), bf(w_in[l]))
        mixer_tail = (wpool_bd, row(pool_scale[l]), conv_w[l])
        wo = bf(w_o[l])

        xp = _ffn(xp, *f1, tm=tm_prompt)
        kp, vp, q, kt, va, pc, kmean, ptail, ctail = _proj_prompt(
            xp.reshape(b, s, d), *mixer_w, cos_p, sin_p, *mixer_tail, n_heads=n_heads)
        a = _attn_prompt(q, kt, va, _gate_table(kmean, n_heads))
        last = l == depth - 1
        xp = _ffn(xp, *f2, mix=(a.reshape(b * s, wa), pc.reshape(b * s, -1), wo),
                  final=row(norm_final) if last else None, tm=tm_prompt)
        outs["kp"].append(kp.transpose(0, 3, 1, 2))
        outs["vp"].append(vp.transpose(0, 3, 1, 2))
        outs["pp"].append(ptail[:, POOL_CARRY - POOL_STATE:])
        outs["cp"].append(ctail[:, CONV_CARRY - (CONV_K - 1):])

        xs = _ffn(xs, *f1, tm=db)
        qs, ksn, vsn, pcs, us, zs = _proj_sample(
            xs, *mixer_w, cos_s, sin_s, *mixer_tail, state_pool[l], state_conv[l],
            n_heads=n_heads, pos=past_len)
        col = lambda a: a.reshape(db, n_heads, HEAD_DIM, 1)
        idx = _sample_gates(cache_kt, page_table, col(qs), layer=l)
        a_s = _sample_attn(cache_kt, cache_vt, page_table, idx[:, :, :MOBA_TOPK],
                           col(qs), col(ksn), col(vsn), layer=l)
        xs = _ffn(xs, *f2, mix=(bf(a_s.reshape(db, wa)), pcs, wo),
                  final=row(norm_final) if last else None, tm=db)
        outs["ks"].append(ksn.reshape(db, 1, n_heads, HEAD_DIM))
        outs["vs"].append(vsn.reshape(db, 1, n_heads, HEAD_DIM))
        outs["ps"].append(jnp.concatenate([state_pool[l][:, 1:], us[:, None]], axis=1))
        outs["cs"].append(jnp.concatenate([state_conv[l][:, 1:], zs[:, None]], axis=1))

    st = lambda n: jnp.stack(outs[n])
    return (xp.reshape(b, s, d), xs.reshape(db, 1, d), st("kp"), st("vp"), st("ks"), st("vs"),
            st("pp"), st("ps"), st("cp"), st("cs"))
```

```python
import functools

import jax
import jax.numpy as jnp
from jax import lax
from jax.experimental import pallas as pl
from jax.experimental.pallas import tpu as pltpu

F32 = jnp.float32
BF16 = jnp.bfloat16

HEAD_DIM = 64
LANES = 128
MXU_DIM = 256
MOBA_BLOCK = 256
MOBA_TOPK = 3
POOL_WINDOWS = (2, 4, 8, 16)
POOL_STATE = max(POOL_WINDOWS) - 1
POOL_CARRY = 16
CONV_K = 3
CONV_CARRY = 8
ROPE_THETA = 10000.0
RMS_EPS = 1e-6
NEG = -1e30
VMEM_LIMIT = 56 * 1024 * 1024


def _dot(a, b):
    return jnp.dot(a, b, preferred_element_type=F32)


def _rms(x, g):
    return x * lax.rsqrt(jnp.mean(x * x, axis=-1, keepdims=True) + RMS_EPS) * g


def _resident(shape):
    return pl.BlockSpec(shape, lambda *_: (0,) * len(shape), pipeline_mode=pl.Buffered(1))


def _params(sem):
    return pltpu.CompilerParams(dimension_semantics=sem, vmem_limit_bytes=VMEM_LIMIT)


def _ffn_kernel(*refs, premix, final_norm, ff_chunks):
    it = iter(refs)
    x_ref = next(it)
    if premix:
        a_ref, pc_ref, wo_ref = next(it), next(it), next(it)
    n_ref, wg_ref, wu_ref, wd_ref = next(it), next(it), next(it), next(it)
    if final_norm:
        nf_ref = next(it)
    o_ref = next(it)

    x = x_ref[...]
    if premix:
        wa = a_ref.shape[-1]
        x = x + _dot(a_ref[...], wo_ref[:wa, :]) + _dot(pc_ref[...], wo_ref[wa:, :])
    h = _rms(x, n_ref[...]).astype(BF16)
    y = jnp.zeros_like(x)
    for lo, hi in ff_chunks:
        g = _dot(h, wg_ref[:, lo:hi])
        u = _dot(h, wu_ref[:, lo:hi])
        act = (g * jax.nn.sigmoid(g) * u).astype(BF16)
        y = y + _dot(act, wd_ref[lo:hi, :])
    x = x + 0.5 * y
    if final_norm:
        x = _rms(x, nf_ref[...])
    o_ref[...] = x


def _ffn(x, norm, wg, wu, wd, *, mix=None, final=None, tm):
    m, d = x.shape
    ff = wg.shape[1]
    half = -(-(ff // 2) // MXU_DIM) * MXU_DIM
    ff_chunks = ((0, half), (half, ff))
    row = lambda w: pl.BlockSpec((tm, w), lambda i: (i, 0))
    args, specs = [x], [row(d)]
    if mix is not None:
        a, pc, wo = mix
        args += [a, pc, wo]
        specs += [row(a.shape[1]), row(pc.shape[1]), _resident(wo.shape)]
    args += [norm, wg, wu, wd]
    specs += [_resident(norm.shape), _resident(wg.shape), _resident(wu.shape), _resident(wd.shape)]
    if final is not None:
        args.append(final)
        specs.append(_resident(final.shape))
    return pl.pallas_call(
        functools.partial(_ffn_kernel, premix=mix is not None, final_norm=final is not None,
                          ff_chunks=ff_chunks),
        grid=(m // tm,),
        in_specs=specs,
        out_specs=row(d),
        out_shape=jax.ShapeDtypeStruct((m, d), F32),
        compiler_params=_params(("parallel",)),
        name="ffn",
    )(*args)


def _rope_chunk(x, cos, sin_signed):
    lane = lax.broadcasted_iota(jnp.int32, x.shape, 1)
    first_half = (lane % HEAD_DIM) < (HEAD_DIM // 2)
    partner = jnp.where(first_half, pltpu.roll(x, LANES - HEAD_DIM // 2, 1),
                        pltpu.roll(x, HEAD_DIM // 2, 1))
    return x * cos + partner * sin_signed


def _pool_select(s2, s4, s8, s16, pos):
    lane = lax.broadcasted_iota(jnp.int32, s2.shape, 1)
    group = s2.shape[1] // len(POOL_WINDOWS)
    posf = (pos + 1).astype(F32)
    mean = lambda s, w: s / jnp.minimum(posf, float(w))
    return jnp.where(lane < group, mean(s2, 2),
                     jnp.where(lane < 2 * group, mean(s4, 4),
                               jnp.where(lane < 3 * group, mean(s8, 8), mean(s16, 16))))


def _rope_tables(pos):
    half = HEAD_DIM // 2
    inv = jnp.power(ROPE_THETA, -jnp.arange(half, dtype=F32) / half)
    ang = pos.astype(F32)[:, None] * inv[None, :]
    cos, sin = jnp.cos(ang), jnp.sin(ang)
    reps = LANES // HEAD_DIM
    return (jnp.tile(jnp.concatenate([cos, cos], axis=1), (1, reps)),
            jnp.tile(jnp.concatenate([-sin, sin], axis=1), (1, reps)))


def _proj_kernel(x_ref, nm_ref, win_ref, cos_ref, sin_ref, wpool_ref, pscale_ref, convw_ref,
                 *refs, wa, wb, wc):
    (k_ref, v_ref, q_ref, kt_ref, va_ref, pc_ref, km_ref, ptail_ref, ctail_ref,
     prevu_ref, prevz_ref) = refs[-11:]
    i = pl.program_id(1)
    tm = x_ref.shape[1]
    n_chunks = wa // LANES

    @pl.when(i == 0)
    def _():
        prevu_ref[...] = jnp.zeros_like(prevu_ref)
        prevz_ref[...] = jnp.zeros_like(prevz_ref)

    h = _rms(x_ref[0], nm_ref[...]).astype(BF16)
    cos, sin = cos_ref[...], sin_ref[...]
    lane = lax.broadcasted_iota(jnp.int32, (tm, LANES), 1)
    low = lane < HEAD_DIM
    ones_col = (lane == HEAD_DIM).astype(F32)
    block_onehot_t = (lax.broadcasted_iota(jnp.int32, (HEAD_DIM, tm), 0) == i).astype(BF16)
    scale = HEAD_DIM ** -0.5

    wide = {}
    for c in range(n_chunks):
        sl = slice(c * LANES, (c + 1) * LANES)
        if c % (MXU_DIM // LANES) == 0:
            cols = slice(c * LANES, c * LANES + MXU_DIM)
            wide = {name: _dot(h, win_ref[:, off + cols.start:off + cols.stop])
                    for name, off in (("q", 0), ("k", wa), ("v", 2 * wa))}
        part = slice((c * LANES) % MXU_DIM, (c * LANES) % MXU_DIM + LANES)
        q = _rope_chunk(wide["q"][:, part], cos, sin) * scale
        k = _rope_chunk(wide["k"][:, part], cos, sin)
        v = wide["v"][:, part]
        km_ref[0, 0, :, sl] = jnp.mean(k, axis=0, keepdims=True)
        k_t, v_t = k.T, v.T
        for hh in range(2):
            head = 2 * c + hh
            rows = slice(hh * HEAD_DIM, (hh + 1) * HEAD_DIM)
            k_ref[0, 0, head] = k_t[rows]
            v_ref[0, 0, head] = v_t[rows]
            kt_ref[0, head, 0] = jnp.concatenate([k_t[rows].astype(BF16), block_onehot_t], axis=0)
            qh, vh = (q, v) if hh == 0 else (pltpu.roll(q, HEAD_DIM, 1), pltpu.roll(v, HEAD_DIM, 1))
            q_ref[0, head] = jnp.where(low, qh, 0.0).astype(BF16)
            va_ref[0, head, 0] = jnp.where(low, vh, ones_col).astype(BF16)

    o = 3 * wa
    u = _dot(h, win_ref[:, o:o + wb])
    hc = _dot(h, win_ref[:, o + wb:o + wb + wc])
    bg = _dot(h, win_ref[:, o + wb + wc:o + wb + 2 * wc])
    cg = _dot(h, win_ref[:, o + wb + 2 * wc:o + wb + 3 * wc])

    pos = i * tm + lax.broadcasted_iota(jnp.int32, (tm, 1), 0)
    e = jnp.concatenate([prevu_ref[...], u], axis=0)
    s2 = e + pltpu.roll(e, 1, 0)
    s4 = s2 + pltpu.roll(s2, 2, 0)
    s8 = s4 + pltpu.roll(s4, 4, 0)
    s16 = s8 + pltpu.roll(s8, 8, 0)
    t = slice(POOL_CARRY, POOL_CARRY + tm)
    d = _pool_select(s2[t], s4[t], s8[t], s16[t], pos) - u
    p = _dot(d.astype(BF16), wpool_ref[...]) * pscale_ref[...]
    prevu_ref[...] = u[tm - POOL_CARRY:]
    ptail_ref[0] = u[tm - POOL_CARRY:]

    z = cg * hc
    ze = jnp.concatenate([prevz_ref[...], z], axis=0)
    tz = slice(CONV_CARRY, CONV_CARRY + tm)
    conv = (z * convw_ref[2:3, :] + pltpu.roll(ze, 1, 0)[tz] * convw_ref[1:2, :]
            + pltpu.roll(ze, 2, 0)[tz] * convw_ref[0:1, :])
    prevz_ref[...] = z[tm - CONV_CARRY:]
    ctail_ref[0] = z[tm - CONV_CARRY:]
    pc_ref[0, :, :wb] = p.astype(BF16)
    pc_ref[0, :, wb:] = (bg * conv).astype(BF16)


def _proj_prompt(x, nm, w_in, cos, sin, wpool_bd, pscale, conv_w, *, n_heads, layer, depth,
                 kv_all=None):
    b, s, d = x.shape
    tm = MOBA_BLOCK
    nb = s // tm
    wa = n_heads * HEAD_DIM
    wb = wpool_bd.shape[0]
    wc = conv_w.shape[1]
    assert s % tm == 0 and nb <= LANES - HEAD_DIM and wa % MXU_DIM == 0
    assert w_in.shape[1] == 3 * wa + wb + 3 * wc
    seq = lambda w: pl.BlockSpec((1, tm, w), lambda bi, i: (bi, i, 0))
    tab = pl.BlockSpec((tm, LANES), lambda bi, i: (i, 0))
    per_b = lambda r, w: pl.BlockSpec((1, r, w), lambda bi, i: (bi, 0, 0))
    head_t = pl.BlockSpec((1, 1, n_heads, HEAD_DIM, tm), lambda bi, i: (layer, bi, 0, 0, i))
    out_shape = (
        jax.ShapeDtypeStruct((depth, b, n_heads, HEAD_DIM, s), F32),
        jax.ShapeDtypeStruct((depth, b, n_heads, HEAD_DIM, s), F32),
        jax.ShapeDtypeStruct((b, n_heads, s, LANES), BF16),
        jax.ShapeDtypeStruct((b, n_heads, nb, LANES, tm), BF16),
        jax.ShapeDtypeStruct((b, n_heads, nb, tm, LANES), BF16),
        jax.ShapeDtypeStruct((b, s, wb + wc), BF16),
        jax.ShapeDtypeStruct((b, nb, 1, wa), F32),
        jax.ShapeDtypeStruct((b, POOL_CARRY, wb), F32),
        jax.ShapeDtypeStruct((b, CONV_CARRY, wc), F32),
    )
    out_specs = (
        head_t, head_t,
        pl.BlockSpec((1, n_heads, tm, LANES), lambda bi, i: (bi, 0, i, 0)),
        pl.BlockSpec((1, n_heads, 1, LANES, tm), lambda bi, i: (bi, 0, i, 0, 0)),
        pl.BlockSpec((1, n_heads, 1, tm, LANES), lambda bi, i: (bi, 0, i, 0, 0)),
        seq(wb + wc), pl.BlockSpec((1, 1, 1, wa), lambda bi, i: (bi, i, 0, 0)),
        per_b(POOL_CARRY, wb), per_b(CONV_CARRY, wc),
    )
    args = [x, nm, w_in, cos, sin, wpool_bd, pscale, conv_w]
    in_specs = [seq(d), _resident(nm.shape), _resident(w_in.shape), tab, tab,
                _resident(wpool_bd.shape), _resident(pscale.shape), _resident(conv_w.shape)]
    aliases = {}
    if kv_all is not None:
        aliases = {len(args): 0, len(args) + 1: 1}
        args += list(kv_all)
        in_specs += [pl.BlockSpec(memory_space=pl.ANY)] * 2
    return pl.pallas_call(
        functools.partial(_proj_kernel, wa=wa, wb=wb, wc=wc),
        grid=(b, nb),
        in_specs=in_specs,
        out_specs=out_specs,
        out_shape=out_shape,
        scratch_shapes=[pltpu.VMEM((POOL_CARRY, wb), F32), pltpu.VMEM((CONV_CARRY, wc), F32)],
        input_output_aliases=aliases,
        compiler_params=_params(("parallel", "arbitrary")),
        name="proj_prompt",
    )(*args)


def _top_blocks_t(gate_t, n_valid):
    blk = lax.broadcasted_iota(jnp.int32, gate_t.shape, 0)
    blk_f = blk.astype(F32)
    g = jnp.where(blk < n_valid, gate_t, -jnp.inf)
    picked = jnp.zeros(gate_t.shape, jnp.bool_)
    for _ in range(MOBA_TOPK):
        m = jnp.max(g, axis=0, keepdims=True)
        first = jnp.min(jnp.where(g == m, blk_f, float(2 * LANES)), axis=0, keepdims=True)
        hit = (blk_f == first) & (m > -jnp.inf)
        picked = picked | hit
        g = jnp.where(hit, -jnp.inf, g)
    return jnp.where(picked, 0.0, NEG)


def _dot_nt(a, b):
    return lax.dot_general(a, b, (((1,), (1,)), ((), ())), preferred_element_type=F32)


def _logits(a, b):
    return jnp.dot(a, b, preferred_element_type=F32).astype(BF16)


ATTN_BLOCKS_PER_TRIP = 2
ATTN_HEADS_PER_STEP = 8


def _attn_kernel(q_ref, kt_ref, va_ref, kmr_ref, o_ref, qs_ref, m_ref, acc_ref, s_ref):
    i = pl.program_id(2)
    tq = q_ref.shape[2]
    nb = kt_ref.shape[2]
    nbp = -(-nb // 8) * 8
    lane = lax.broadcasted_iota(jnp.int32, (tq, LANES), 1)
    row = lax.broadcasted_iota(jnp.int32, (tq, MOBA_BLOCK), 0)
    col = lax.broadcasted_iota(jnp.int32, (tq, MOBA_BLOCK), 1)
    n_h = q_ref.shape[1]
    g = ATTN_BLOCKS_PER_TRIP
    last_group = nb // g - 1

    def store_logits(slot, jj, hh):
        for t in range(g):
            s_ref[slot, hh, t] = _logits(qs_ref[hh], kt_ref[0, hh, g * jj + t])

    heads = range(n_h)
    gates, diag, probs = [], [], []
    for hh in heads:
        km = kmr_ref[0, hh]
        km_hi = km.astype(BF16)
        km_lo = (km - km_hi.astype(F32)).astype(BF16)
        gates.append(_dot_nt(km_hi, q_ref[0, hh]) + _dot_nt(km_lo, q_ref[0, hh]))
    for hh in heads:
        diag.append(jnp.where(col <= row, _logits(q_ref[0, hh], kt_ref[0, hh, i]), NEG))
    for hh in heads:
        pen_t = _top_blocks_t(gates[hh][HEAD_DIM:HEAD_DIM + nbp], i)
        pen_t = jnp.concatenate([jnp.zeros((HEAD_DIM, tq), F32), pen_t,
                                 jnp.zeros((LANES - HEAD_DIM - nbp, tq), F32)], axis=0)
        qs_ref[hh] = (q_ref[0, hh].astype(F32) + pen_t.T).astype(BF16)
    for hh in heads:
        m = jnp.max(diag[hh], axis=1, keepdims=True)
        m_ref[hh] = m.astype(F32)
        probs.append(jnp.exp(diag[hh] - m))
    for hh in heads:
        acc_ref[hh] = _dot(probs[hh], va_ref[0, hh, i])
    for hh in heads:
        store_logits(0, 0, hh)

    @pl.loop(0, (i + g - 1) // g)
    def _(jj):
        slot = jj % 2
        for hh in range(n_h):
            s = [s_ref[slot, hh, t] for t in range(g)]
            m = m_ref[hh]
            s_max = functools.reduce(jnp.maximum, s)
            m_new = jnp.maximum(m, jnp.max(s_max, axis=1, keepdims=True).astype(F32))
            m_b = m_new.astype(BF16)
            p = jnp.concatenate([jnp.exp(s[t] - m_b) for t in range(g)], axis=1)
            v = va_ref[0, hh, pl.ds(g * jj, g)].reshape(g * tq, LANES)
            acc_ref[hh] = acc_ref[hh] * jnp.exp(m - m_new) + _dot(p, v)
            m_ref[hh] = m_new
        nxt = jnp.minimum(jj + 1, last_group)
        for hh in range(n_h):
            store_logits(1 - slot, nxt, hh)

    outs = [acc_ref[hh] / acc_ref[hh][:, HEAD_DIM:HEAD_DIM + 1] for hh in range(n_h)]
    for p in range(n_h // 2):
        o_ref[0, :, p * LANES:(p + 1) * LANES] = jnp.where(
            lane < HEAD_DIM, outs[2 * p], pltpu.roll(outs[2 * p + 1], HEAD_DIM, 1)).astype(BF16)


def _attn_prompt(q, kt, va, kmt):
    b, n_heads, s, _ = q.shape
    nb = kt.shape[2]
    tq = MOBA_BLOCK
    hg = ATTN_HEADS_PER_STEP
    assert nb % ATTN_BLOCKS_PER_TRIP == 0 and n_heads % hg == 0 and hg % 2 == 0
    return pl.pallas_call(
        _attn_kernel,
        grid=(b, n_heads // hg, nb),
        in_specs=[
            pl.BlockSpec((1, hg, tq, LANES), lambda bi, hp, i: (bi, hp, i, 0)),
            pl.BlockSpec((1, hg, nb, LANES, tq), lambda bi, hp, i: (bi, hp, 0, 0, 0),
                         pipeline_mode=pl.Buffered(1)),
            pl.BlockSpec((1, hg, nb, tq, LANES), lambda bi, hp, i: (bi, hp, 0, 0, 0),
                         pipeline_mode=pl.Buffered(1)),
            pl.BlockSpec((1, hg, LANES, LANES), lambda bi, hp, i: (bi, hp, 0, 0)),
        ],
        out_specs=pl.BlockSpec((1, tq, hg * HEAD_DIM), lambda bi, hp, i: (bi, i, hp)),
        out_shape=jax.ShapeDtypeStruct((b, s, n_heads * HEAD_DIM), BF16),
        scratch_shapes=[
            pltpu.VMEM((hg, tq, LANES), BF16),
            pltpu.VMEM((hg, tq, 1), F32),
            pltpu.VMEM((hg, tq, LANES), F32),
            pltpu.VMEM((2, hg, ATTN_BLOCKS_PER_TRIP, tq, MOBA_BLOCK), BF16),
        ],
        compiler_params=_params(("parallel", "parallel", "arbitrary")),
        name="attn_prompt",
    )(q, kt, va, kmt)


def _gate_table(kmean, n_heads):
    b, nb = kmean.shape[:2]
    t = kmean.reshape(b, nb, n_heads, HEAD_DIM).transpose(0, 2, 1, 3)
    return jnp.pad(t, ((0, 0), (0, 0), (HEAD_DIM, LANES - HEAD_DIM - nb), (0, LANES - HEAD_DIM)))


def _sproj_kernel(x_ref, nm_ref, win_ref, cos_ref, sin_ref, wpool_ref, pscale_ref, convw_ref,
                  spool_ref, sconv_ref, q_ref, k_ref, v_ref, pc_ref, u_ref, z_ref,
                  *, wa, wb, wc, pos):
    h = _rms(x_ref[...], nm_ref[...]).astype(BF16)
    cos, sin = cos_ref[...], sin_ref[...]
    scale = HEAD_DIM ** -0.5
    for c in range(wa // LANES):
        sl = slice(c * LANES, (c + 1) * LANES)
        q_ref[:, sl] = _rope_chunk(_dot(h, win_ref[:, sl]), cos, sin) * scale
        k_ref[:, sl] = _rope_chunk(_dot(h, win_ref[:, wa + c * LANES: wa + (c + 1) * LANES]), cos, sin)
        v_ref[:, sl] = _dot(h, win_ref[:, 2 * wa + c * LANES: 2 * wa + (c + 1) * LANES])
    o = 3 * wa
    u = _dot(h, win_ref[:, o:o + wb])
    hc = _dot(h, win_ref[:, o + wb:o + wb + wc])
    bg = _dot(h, win_ref[:, o + wb + wc:o + wb + 2 * wc])
    cg = _dot(h, win_ref[:, o + wb + 2 * wc:o + wb + 3 * wc])

    sums, run = {}, u
    for r in range(1, max(POOL_WINDOWS)):
        run = run + spool_ref[:, POOL_STATE - r, :]
        if r + 1 in POOL_WINDOWS:
            sums[r + 1] = run
    posv = jnp.full((u.shape[0], 1), pos, jnp.int32)
    d = _pool_select(sums[2], sums[4], sums[8], sums[16], posv) - u
    p = _dot(d.astype(BF16), wpool_ref[...]) * pscale_ref[...]

    z = cg * hc
    conv = (z * convw_ref[2:3, :] + sconv_ref[:, 1, :] * convw_ref[1:2, :]
            + sconv_ref[:, 0, :] * convw_ref[0:1, :])
    u_ref[...] = u
    z_ref[...] = z
    pc_ref[:, :wb] = p.astype(BF16)
    pc_ref[:, wb:] = (bg * conv).astype(BF16)


def _proj_sample(x, nm, w_in, cos, sin, wpool_bd, pscale, conv_w, spool, sconv, *, n_heads, pos):
    m, _ = x.shape
    wa = n_heads * HEAD_DIM
    wb = wpool_bd.shape[0]
    wc = conv_w.shape[1]
    assert spool.shape[1] == POOL_STATE and sconv.shape[1] == CONV_K - 1
    f = lambda w, dt: jax.ShapeDtypeStruct((m, w), dt)
    return pl.pallas_call(
        functools.partial(_sproj_kernel, wa=wa, wb=wb, wc=wc, pos=pos),
        out_shape=(f(wa, F32), f(wa, F32), f(wa, F32), f(wb + wc, BF16), f(wb, F32), f(wc, F32)),
        compiler_params=pltpu.CompilerParams(vmem_limit_bytes=VMEM_LIMIT),
        name="proj_sample",
    )(x, nm, w_in, cos, sin, wpool_bd, pscale, conv_w, spool, sconv)


PAGES_PER_STEP = 16


def _sgate_kernel(pt_ref, *refs, pages_per_block, n_blocks):
    page_refs = refs[:PAGES_PER_STEP]
    q_ref, idx_ref, gate_ref = refs[PAGES_PER_STEP:]
    g = pl.program_id(1)
    blocks_per_step = PAGES_PER_STEP // pages_per_block
    n_heads = q_ref.shape[1]
    lane = lax.broadcasted_iota(jnp.int32, (n_heads, LANES), 1)

    @pl.when(g == 0)
    def _():
        gate_ref[...] = jnp.zeros_like(gate_ref)

    q = jnp.broadcast_to(q_ref[0], page_refs[0].shape[2:])
    gate = gate_ref[...]
    for r in range(blocks_per_step):
        tot = page_refs[r * pages_per_block][0, 0]
        for t in range(1, pages_per_block):
            tot = tot + page_refs[r * pages_per_block + t][0, 0]
        per_token = jnp.sum(tot * q, axis=1)
        val = jnp.sum(per_token, axis=1, keepdims=True) / float(MOBA_BLOCK)
        gate = jnp.where(lane == g * blocks_per_step + r, val, gate)
    gate_ref[...] = gate

    @pl.when(g == pl.num_programs(1) - 1)
    def _():
        lane_f = lane.astype(F32)
        gt = jnp.where(lane < n_blocks, gate, -jnp.inf)
        picks = jnp.zeros((n_heads, LANES), F32)
        for r in range(MOBA_TOPK):
            mx = jnp.max(gt, axis=1, keepdims=True)
            first = jnp.min(jnp.where(gt == mx, lane_f, float(LANES)), axis=1, keepdims=True)
            gt = jnp.where(lane_f == first, -jnp.inf, gt)
            picks = jnp.where(lane == r, first, picks)
        idx_ref[0] = picks.astype(jnp.int32)


def _sample_gates(cache_kt, page_table, q4, *, layer):
    _, n_pool, n_heads, _, page_size = cache_kt.shape
    b, n_pages = page_table.shape
    ppb = MOBA_BLOCK // page_size
    nbp = n_pages // ppb
    assert MOBA_BLOCK % page_size == 0 and PAGES_PER_STEP % ppb == 0
    assert n_pages % PAGES_PER_STEP == 0 and MOBA_TOPK <= nbp <= LANES
    page_spec = lambda r: pl.BlockSpec(
        (1, 1, n_heads, HEAD_DIM, page_size),
        lambda bi, g, pt: (layer, pt[bi, g * PAGES_PER_STEP + r], 0, 0, 0))
    return pl.pallas_call(
        functools.partial(_sgate_kernel, pages_per_block=ppb, n_blocks=nbp),
        grid_spec=pltpu.PrefetchScalarGridSpec(
            num_scalar_prefetch=1,
            grid=(b, n_pages // PAGES_PER_STEP),
            in_specs=[page_spec(r) for r in range(PAGES_PER_STEP)]
            + [pl.BlockSpec((1, n_heads, HEAD_DIM, 1), lambda bi, g, pt: (bi, 0, 0, 0))],
            out_specs=pl.BlockSpec((1, n_heads, LANES), lambda bi, g, pt: (bi, 0, 0)),
            scratch_shapes=[pltpu.VMEM((n_heads, LANES), F32)],
        ),
        out_shape=jax.ShapeDtypeStruct((b, n_heads, LANES), jnp.int32),
        compiler_params=_params(("parallel", "arbitrary")),
        name="sample_gates",
    )(page_table, *([cache_kt] * PAGES_PER_STEP), q4)


def _sattn_kernel(idx_ref, pt_ref, *refs, n_fetch):
    k_refs = refs[:2 * n_fetch]
    v_refs = refs[2 * n_fetch:4 * n_fetch]
    q_ref, kn_ref, vn_ref, o_ref = refs[4 * n_fetch:]
    for hh in range(2):
        q = q_ref[0, hh]
        s_self = jnp.sum(q * kn_ref[0, hh], axis=0, keepdims=True)
        scores = [jnp.sum(k_refs[hh * n_fetch + t][0, 0, 0] * q, axis=0, keepdims=True)
                  for t in range(n_fetch)]
        m = s_self
        for s in scores:
            m = jnp.maximum(m, jnp.max(s, axis=1, keepdims=True))
        p_self = jnp.exp(s_self - m)
        l = p_self
        acc = p_self * vn_ref[0, hh]
        for t, s in enumerate(scores):
            p = jnp.exp(s - m)
            l = l + jnp.sum(p, axis=1, keepdims=True)
            acc = acc + jnp.sum(v_refs[hh * n_fetch + t][0, 0, 0] * p, axis=1, keepdims=True)
        o_ref[0, hh] = acc / l


def _sample_attn(cache_kt, cache_vt, page_table, idx, q4, kn4, vn4, *, layer):
    _, n_pool, n_heads, _, page_size = cache_kt.shape
    b = page_table.shape[0]
    ppb = MOBA_BLOCK // page_size
    n_fetch = MOBA_TOPK * ppb

    def page_spec(hh, r, t):
        def index(bi, hp, idx_s, pt):
            head = 2 * hp + hh
            blk = idx_s[(bi * n_heads + head) * MOBA_TOPK + r]
            return (layer, pt[bi, blk * ppb + t], head, 0, 0)
        return pl.BlockSpec((1, 1, 1, HEAD_DIM, page_size), index)

    fetch = [page_spec(hh, r, t) for hh in range(2) for r in range(MOBA_TOPK) for t in range(ppb)]
    vec = pl.BlockSpec((1, 2, HEAD_DIM, 1), lambda bi, hp, idx_s, pt: (bi, hp, 0, 0))
    return pl.pallas_call(
        functools.partial(_sattn_kernel, n_fetch=n_fetch),
        grid_spec=pltpu.PrefetchScalarGridSpec(
            num_scalar_prefetch=2,
            grid=(b, n_heads // 2),
            in_specs=fetch + fetch + [vec, vec, vec],
            out_specs=vec,
        ),
        out_shape=jax.ShapeDtypeStruct((b, n_heads, HEAD_DIM, 1), F32),
        compiler_params=_params(("parallel", "arbitrary")),
        name="sample_attn",
    )(idx.reshape(-1), page_table, *([cache_kt] * len(fetch)), *([cache_vt] * len(fetch)),
      q4, kn4, vn4)


def kernel(x_prompt, x_sample, cache_k, cache_v, page_table, state_pool, state_conv, norm_ffn1, ffn1_gate, ffn1_up, ffn1_down, norm_mix, w_in, w_pool, pool_scale, conv_w, w_o, norm_ffn2, ffn2_gate, ffn2_up, ffn2_down, norm_final):
    b, s, d = x_prompt.shape
    db, t_new, _ = x_sample.shape
    depth, n_pool, page_size, n_heads, head_dim = cache_k.shape
    assert head_dim == HEAD_DIM and t_new == 1 and n_heads % 2 == 0
    past_len = page_table.shape[1] * page_size
    assert past_len % MOBA_BLOCK == 0 and past_len + 1 >= max(POOL_WINDOWS)
    wa = n_heads * HEAD_DIM
    tm_prompt = 512 if (b * s) % 512 == 0 else MOBA_BLOCK

    cos_p, sin_p = _rope_tables(jnp.arange(s, dtype=jnp.int32))
    cos_s, sin_s = _rope_tables(past_len + jnp.arange(1, dtype=jnp.int32))
    row = lambda a: a.reshape(1, -1)
    bf = lambda a: a.astype(BF16)

    cache_kt = cache_k.transpose(0, 1, 3, 4, 2)
    cache_vt = cache_v.transpose(0, 1, 3, 4, 2)

    xp = x_prompt.reshape(b * s, d)
    xs = x_sample.reshape(db, d)
    outs = {n: [] for n in ("ks", "vs", "pp", "ps", "cp", "cs")}
    kv_all = None
    for l in range(depth):
        n_groups = w_pool.shape[1]
        wpool_bd = bf(jax.scipy.linalg.block_diag(*[w_pool[l, g] for g in range(n_groups)]))
        f1 = (row(norm_ffn1[l]), bf(ffn1_gate[l]), bf(ffn1_up[l]), bf(ffn1_down[l]))
        f2 = (row(norm_ffn2[l]), bf(ffn2_gate[l]), bf(ffn2_up[l]), bf(ffn2_down[l]))
        mixer_w = (row(norm_mix[l]), bf(w_in[l]))
        mixer_tail = (wpool_bd, row(pool_scale[l]), conv_w[l])
        wo = bf(w_o[l])

        xp = _ffn(xp, *f1, tm=tm_prompt)
        kp, vp, q, kt, va, pc, kmean, ptail, ctail = _proj_prompt(
            xp.reshape(b, s, d), *mixer_w, cos_p, sin_p, *mixer_tail, n_heads=n_heads,
            layer=l, depth=depth, kv_all=kv_all)
        kv_all = (kp, vp)
        a = _attn_prompt(q, kt, va, _gate_table(kmean, n_heads))
        last = l == depth - 1
        xp = _ffn(xp, *f2, mix=(a.reshape(b * s, wa), pc.reshape(b * s, -1), wo),
                  final=row(norm_final) if last else None, tm=tm_prompt)
        outs["pp"].append(ptail[:, POOL_CARRY - POOL_STATE:])
        outs["cp"].append(ctail[:, CONV_CARRY - (CONV_K - 1):])

        xs = _ffn(xs, *f1, tm=db)
        qs, ksn, vsn, pcs, us, zs = _proj_sample(
            xs, *mixer_w, cos_s, sin_s, *mixer_tail, state_pool[l], state_conv[l],
            n_heads=n_heads, pos=past_len)
        col = lambda a: a.reshape(db, n_heads, HEAD_DIM, 1)
        idx = _sample_gates(cache_kt, page_table, col(qs), layer=l)
        a_s = _sample_attn(cache_kt, cache_vt, page_table, idx[:, :, :MOBA_TOPK],
                           col(qs), col(ksn), col(vsn), layer=l)
        xs = _ffn(xs, *f2, mix=(bf(a_s.reshape(db, wa)), pcs, wo),
                  final=row(norm_final) if last else None, tm=db)
        outs["ks"].append(ksn.reshape(db, 1, n_heads, HEAD_DIM))
        outs["vs"].append(vsn.reshape(db, 1, n_heads, HEAD_DIM))
        outs["ps"].append(jnp.concatenate([state_pool[l][:, 1:], us[:, None]], axis=1))
        outs["cs"].append(jnp.concatenate([state_conv[l][:, 1:], zs[:, None]], axis=1))

    st = lambda n: jnp.stack(outs[n])
    k_prompt, v_prompt = (a.transpose(0, 1, 4, 2, 3) for a in kv_all)
    return (xp.reshape(b, s, d), xs.reshape(db, 1, d), k_prompt, v_prompt, st("ks"), st("vs"),
            st("pp"), st("ps"), st("cp"), st("cs"))
```

```python
import functools

import jax
import jax.numpy as jnp
from jax import lax
from jax.experimental import pallas as pl
from jax.experimental.pallas import tpu as pltpu

F32 = jnp.float32
BF16 = jnp.bfloat16

HEAD_DIM = 64
LANES = 128
MXU_DIM = 256
MOBA_BLOCK = 256
MOBA_TOPK = 3
POOL_WINDOWS = (2, 4, 8, 16)
POOL_STATE = max(POOL_WINDOWS) - 1
POOL_CARRY = 16
CONV_K = 3
CONV_CARRY = 8
ROPE_THETA = 10000.0
RMS_EPS = 1e-6
NEG = -1e30
VMEM_LIMIT = 56 * 1024 * 1024
ATTN_VMEM_LIMIT = 62 * 1024 * 1024


def _dot(a, b):
    return jnp.dot(a, b, preferred_element_type=F32)


def _rms(x, g):
    return x * lax.rsqrt(jnp.mean(x * x, axis=-1, keepdims=True) + RMS_EPS) * g


def _resident(shape):
    return pl.BlockSpec(shape, lambda *_: (0,) * len(shape), pipeline_mode=pl.Buffered(1))


def _params(sem):
    return pltpu.CompilerParams(dimension_semantics=sem, vmem_limit_bytes=VMEM_LIMIT)


def _ffn_kernel(*refs, premix, final_norm, ff_chunks):
    it = iter(refs)
    x_ref = next(it)
    if premix:
        a_ref, pc_ref, wo_ref = next(it), next(it), next(it)
    n_ref, wg_ref, wu_ref, wd_ref = next(it), next(it), next(it), next(it)
    if final_norm:
        nf_ref = next(it)
    o_ref = next(it)

    x = x_ref[...]
    if premix:
        wa = a_ref.shape[-1]
        x = x + _dot(a_ref[...], wo_ref[:wa, :]) + _dot(pc_ref[...], wo_ref[wa:, :])
    h = _rms(x, n_ref[...]).astype(BF16)
    y = jnp.zeros_like(x)
    for lo, hi in ff_chunks:
        g = _dot(h, wg_ref[:, lo:hi])
        u = _dot(h, wu_ref[:, lo:hi])
        act = (g * jax.nn.sigmoid(g) * u).astype(BF16)
        y = y + _dot(act, wd_ref[lo:hi, :])
    x = x + 0.5 * y
    if final_norm:
        x = _rms(x, nf_ref[...])
    o_ref[...] = x


def _ffn(x, norm, wg, wu, wd, *, mix=None, final=None, tm):
    m, d = x.shape
    ff = wg.shape[1]
    half = -(-(ff // 2) // MXU_DIM) * MXU_DIM
    ff_chunks = ((0, half), (half, ff))
    row = lambda w: pl.BlockSpec((tm, w), lambda i: (i, 0))
    args, specs = [x], [row(d)]
    if mix is not None:
        a, pc, wo = mix
        args += [a, pc, wo]
        specs += [row(a.shape[1]), row(pc.shape[1]), _resident(wo.shape)]
    args += [norm, wg, wu, wd]
    specs += [_resident(norm.shape), _resident(wg.shape), _resident(wu.shape), _resident(wd.shape)]
    if final is not None:
        args.append(final)
        specs.append(_resident(final.shape))
    return pl.pallas_call(
        functools.partial(_ffn_kernel, premix=mix is not None, final_norm=final is not None,
                          ff_chunks=ff_chunks),
        grid=(m // tm,),
        in_specs=specs,
        out_specs=row(d),
        out_shape=jax.ShapeDtypeStruct((m, d), F32),
        compiler_params=_params(("parallel",)),
        name="ffn",
    )(*args)


def _rope_chunk(x, cos, sin_signed):
    lane = lax.broadcasted_iota(jnp.int32, x.shape, 1)
    first_half = (lane % HEAD_DIM) < (HEAD_DIM // 2)
    partner = jnp.where(first_half, pltpu.roll(x, LANES - HEAD_DIM // 2, 1),
                        pltpu.roll(x, HEAD_DIM // 2, 1))
    return x * cos + partner * sin_signed


def _pool_select(s2, s4, s8, s16, pos):
    lane = lax.broadcasted_iota(jnp.int32, s2.shape, 1)
    group = s2.shape[1] // len(POOL_WINDOWS)
    posf = (pos + 1).astype(F32)
    mean = lambda s, w: s / jnp.minimum(posf, float(w))
    return jnp.where(lane < group, mean(s2, 2),
                     jnp.where(lane < 2 * group, mean(s4, 4),
                               jnp.where(lane < 3 * group, mean(s8, 8), mean(s16, 16))))


def _rope_tables(pos):
    half = HEAD_DIM // 2
    inv = jnp.power(ROPE_THETA, -jnp.arange(half, dtype=F32) / half)
    ang = pos.astype(F32)[:, None] * inv[None, :]
    cos, sin = jnp.cos(ang), jnp.sin(ang)
    reps = LANES // HEAD_DIM
    return (jnp.tile(jnp.concatenate([cos, cos], axis=1), (1, reps)),
            jnp.tile(jnp.concatenate([-sin, sin], axis=1), (1, reps)))


def _proj_kernel(x_ref, nm_ref, win_ref, cos_ref, sin_ref, wpool_ref, pscale_ref, convw_ref,
                 *refs, wa, wb, wc):
    (k_ref, v_ref, q_ref, kt_ref, va_ref, pc_ref, km_ref, ptail_ref, ctail_ref,
     prevu_ref, prevz_ref) = refs[-11:]
    i = pl.program_id(1)
    tm = x_ref.shape[1]
    n_chunks = wa // LANES

    @pl.when(i == 0)
    def _():
        prevu_ref[...] = jnp.zeros_like(prevu_ref)
        prevz_ref[...] = jnp.zeros_like(prevz_ref)

    h = _rms(x_ref[0], nm_ref[...]).astype(BF16)
    cos, sin = cos_ref[...], sin_ref[...]
    lane = lax.broadcasted_iota(jnp.int32, (tm, LANES), 1)
    low = lane < HEAD_DIM
    ones_col = (lane == HEAD_DIM).astype(F32)
    block_onehot_t = (lax.broadcasted_iota(jnp.int32, (HEAD_DIM, tm), 0) == i).astype(BF16)
    scale = HEAD_DIM ** -0.5

    wide = {}
    for c in range(n_chunks):
        sl = slice(c * LANES, (c + 1) * LANES)
        if c % (MXU_DIM // LANES) == 0:
            cols = slice(c * LANES, c * LANES + MXU_DIM)
            wide = {name: _dot(h, win_ref[:, off + cols.start:off + cols.stop])
                    for name, off in (("q", 0), ("k", wa), ("v", 2 * wa))}
        part = slice((c * LANES) % MXU_DIM, (c * LANES) % MXU_DIM + LANES)
        q = _rope_chunk(wide["q"][:, part], cos, sin) * scale
        k = _rope_chunk(wide["k"][:, part], cos, sin)
        v = wide["v"][:, part]
        km_ref[0, 0, :, sl] = jnp.mean(k, axis=0, keepdims=True)
        k_t, v_t = k.T, v.T
        for hh in range(2):
            head = 2 * c + hh
            rows = slice(hh * HEAD_DIM, (hh + 1) * HEAD_DIM)
            k_ref[0, 0, head] = k_t[rows]
            v_ref[0, 0, head] = v_t[rows]
            kt_ref[0, head, 0] = jnp.concatenate([k_t[rows].astype(BF16), block_onehot_t], axis=0)
            qh, vh = (q, v) if hh == 0 else (pltpu.roll(q, HEAD_DIM, 1), pltpu.roll(v, HEAD_DIM, 1))
            q_ref[0, head] = jnp.where(low, qh, 0.0).astype(BF16)
            va_ref[0, head, 0] = jnp.where(low, vh, ones_col).astype(BF16)

    o = 3 * wa
    u = _dot(h, win_ref[:, o:o + wb])
    hc = _dot(h, win_ref[:, o + wb:o + wb + wc])
    bg = _dot(h, win_ref[:, o + wb + wc:o + wb + 2 * wc])
    cg = _dot(h, win_ref[:, o + wb + 2 * wc:o + wb + 3 * wc])

    pos = i * tm + lax.broadcasted_iota(jnp.int32, (tm, 1), 0)
    e = jnp.concatenate([prevu_ref[...], u], axis=0)
    s2 = e + pltpu.roll(e, 1, 0)
    s4 = s2 + pltpu.roll(s2, 2, 0)
    s8 = s4 + pltpu.roll(s4, 4, 0)
    s16 = s8 + pltpu.roll(s8, 8, 0)
    t = slice(POOL_CARRY, POOL_CARRY + tm)
    d = _pool_select(s2[t], s4[t], s8[t], s16[t], pos) - u
    p = _dot(d.astype(BF16), wpool_ref[...]) * pscale_ref[...]
    prevu_ref[...] = u[tm - POOL_CARRY:]
    ptail_ref[0] = u[tm - POOL_CARRY:]

    z = cg * hc
    ze = jnp.concatenate([prevz_ref[...], z], axis=0)
    tz = slice(CONV_CARRY, CONV_CARRY + tm)
    conv = (z * convw_ref[2:3, :] + pltpu.roll(ze, 1, 0)[tz] * convw_ref[1:2, :]
            + pltpu.roll(ze, 2, 0)[tz] * convw_ref[0:1, :])
    prevz_ref[...] = z[tm - CONV_CARRY:]
    ctail_ref[0] = z[tm - CONV_CARRY:]
    pc_ref[0, :, :wb] = p.astype(BF16)
    pc_ref[0, :, wb:] = (bg * conv).astype(BF16)


def _proj_prompt(x, nm, w_in, cos, sin, wpool_bd, pscale, conv_w, *, n_heads, layer, depth,
                 kv_all=None):
    b, s, d = x.shape
    tm = MOBA_BLOCK
    nb = s // tm
    wa = n_heads * HEAD_DIM
    wb = wpool_bd.shape[0]
    wc = conv_w.shape[1]
    assert s % tm == 0 and nb <= LANES - HEAD_DIM and wa % MXU_DIM == 0
    assert w_in.shape[1] == 3 * wa + wb + 3 * wc
    seq = lambda w: pl.BlockSpec((1, tm, w), lambda bi, i: (bi, i, 0))
    tab = pl.BlockSpec((tm, LANES), lambda bi, i: (i, 0))
    per_b = lambda r, w: pl.BlockSpec((1, r, w), lambda bi, i: (bi, 0, 0))
    head_t = pl.BlockSpec((1, 1, n_heads, HEAD_DIM, tm), lambda bi, i: (layer, bi, 0, 0, i))
    out_shape = (
        jax.ShapeDtypeStruct((depth, b, n_heads, HEAD_DIM, s), F32),
        jax.ShapeDtypeStruct((depth, b, n_heads, HEAD_DIM, s), F32),
        jax.ShapeDtypeStruct((b, n_heads, s, LANES), BF16),
        jax.ShapeDtypeStruct((b, n_heads, nb, LANES, tm), BF16),
        jax.ShapeDtypeStruct((b, n_heads, nb, tm, LANES), BF16),
        jax.ShapeDtypeStruct((b, s, wb + wc), BF16),
        jax.ShapeDtypeStruct((b, nb, 1, wa), F32),
        jax.ShapeDtypeStruct((b, POOL_CARRY, wb), F32),
        jax.ShapeDtypeStruct((b, CONV_CARRY, wc), F32),
    )
    out_specs = (
        head_t, head_t,
        pl.BlockSpec((1, n_heads, tm, LANES), lambda bi, i: (bi, 0, i, 0)),
        pl.BlockSpec((1, n_heads, 1, LANES, tm), lambda bi, i: (bi, 0, i, 0, 0)),
        pl.BlockSpec((1, n_heads, 1, tm, LANES), lambda bi, i: (bi, 0, i, 0, 0)),
        seq(wb + wc), pl.BlockSpec((1, 1, 1, wa), lambda bi, i: (bi, i, 0, 0)),
        per_b(POOL_CARRY, wb), per_b(CONV_CARRY, wc),
    )
    args = [x, nm, w_in, cos, sin, wpool_bd, pscale, conv_w]
    in_specs = [seq(d), _resident(nm.shape), _resident(w_in.shape), tab, tab,
                _resident(wpool_bd.shape), _resident(pscale.shape), _resident(conv_w.shape)]
    aliases = {}
    if kv_all is not None:
        aliases = {len(args): 0, len(args) + 1: 1}
        args += list(kv_all)
        in_specs += [pl.BlockSpec(memory_space=pl.ANY)] * 2
    return pl.pallas_call(
        functools.partial(_proj_kernel, wa=wa, wb=wb, wc=wc),
        grid=(b, nb),
        in_specs=in_specs,
        out_specs=out_specs,
        out_shape=out_shape,
        scratch_shapes=[pltpu.VMEM((POOL_CARRY, wb), F32), pltpu.VMEM((CONV_CARRY, wc), F32)],
        input_output_aliases=aliases,
        compiler_params=_params(("parallel", "arbitrary")),
        name="proj_prompt",
    )(*args)


def _top_blocks_t(gate_t, n_valid):
    blk = lax.broadcasted_iota(jnp.int32, gate_t.shape, 0)
    blk_f = blk.astype(F32)
    g = jnp.where(blk < n_valid, gate_t, -jnp.inf)
    picked = jnp.zeros(gate_t.shape, jnp.bool_)
    for _ in range(MOBA_TOPK):
        m = jnp.max(g, axis=0, keepdims=True)
        first = jnp.min(jnp.where(g == m, blk_f, float(2 * LANES)), axis=0, keepdims=True)
        hit = (blk_f == first) & (m > -jnp.inf)
        picked = picked | hit
        g = jnp.where(hit, -jnp.inf, g)
    return jnp.where(picked, 0.0, NEG)


def _dot_nt(a, b):
    return lax.dot_general(a, b, (((1,), (1,)), ((), ())), preferred_element_type=F32)


def _logits(a, b):
    return jnp.dot(a, b, preferred_element_type=F32).astype(BF16)


ATTN_BLOCKS_PER_TRIP = 2
ATTN_HEADS_PER_STEP = 8


def _decode_gates_step(page_refs, dq_ref, gate_ref, lin, *, pages_per_block, n_pages, n_groups):
    n_step = len(page_refs)
    n_heads = dq_ref.shape[1]

    @pl.when(lin < n_groups)
    def _():
        first_page = lax.rem(lin * n_step, n_pages)
        first_block = lax.div(first_page, pages_per_block)
        lane = lax.broadcasted_iota(jnp.int32, (n_heads, LANES), 1)

        @pl.when(first_page == 0)
        def _():
            gate_ref[...] = jnp.zeros_like(gate_ref)

        q = jnp.broadcast_to(dq_ref[0], page_refs[0].shape[2:])
        gate = gate_ref[0]
        for r in range(n_step // pages_per_block):
            tot = page_refs[r * pages_per_block][0, 0]
            for t in range(1, pages_per_block):
                tot = tot + page_refs[r * pages_per_block + t][0, 0]
            per_token = jnp.sum(tot * q, axis=1)
            val = jnp.sum(per_token, axis=1, keepdims=True) / float(MOBA_BLOCK)
            gate = jnp.where(lane == first_block + r, val, gate)
        gate_ref[0] = gate


def _attn_kernel(pt_ref, q_ref, kt_ref, va_ref, kmr_ref, *refs, decode):
    n_step = decode["n_step"]
    page_refs, dq_ref = refs[:n_step], refs[n_step]
    o_ref, gate_ref, qs_ref, m_ref, acc_ref, s_ref = refs[n_step + 1:]
    lin = (pl.program_id(0) * pl.num_programs(1) + pl.program_id(1)) * pl.num_programs(2) \
        + pl.program_id(2)
    _decode_gates_step(page_refs, dq_ref, gate_ref, lin, pages_per_block=decode["ppb"],
                       n_pages=decode["n_pages"], n_groups=decode["n_groups"])

    i = pl.program_id(2)
    tq = q_ref.shape[2]
    nb = kt_ref.shape[2]
    nbp = -(-nb // 8) * 8
    lane = lax.broadcasted_iota(jnp.int32, (tq, LANES), 1)
    row = lax.broadcasted_iota(jnp.int32, (tq, MOBA_BLOCK), 0)
    col = lax.broadcasted_iota(jnp.int32, (tq, MOBA_BLOCK), 1)
    n_h = q_ref.shape[1]
    g = ATTN_BLOCKS_PER_TRIP
    last_group = nb // g - 1

    def store_logits(slot, jj, hh):
        for t in range(g):
            s_ref[slot, hh, t] = _logits(qs_ref[hh], kt_ref[0, hh, g * jj + t])

    heads = range(n_h)
    gates, diag, probs = [], [], []
    for hh in heads:
        km = kmr_ref[0, hh]
        km_hi = km.astype(BF16)
        km_lo = (km - km_hi.astype(F32)).astype(BF16)
        gates.append(_dot_nt(km_hi, q_ref[0, hh]) + _dot_nt(km_lo, q_ref[0, hh]))
    for hh in heads:
        diag.append(jnp.where(col <= row, _logits(q_ref[0, hh], kt_ref[0, hh, i]), NEG))
    for hh in heads:
        pen_t = _top_blocks_t(gates[hh][HEAD_DIM:HEAD_DIM + nbp], i)
        pen_t = jnp.concatenate([jnp.zeros((HEAD_DIM, tq), F32), pen_t,
                                 jnp.zeros((LANES - HEAD_DIM - nbp, tq), F32)], axis=0)
        qs_ref[hh] = (q_ref[0, hh].astype(F32) + pen_t.T).astype(BF16)
    for hh in heads:
        m = jnp.max(diag[hh], axis=1, keepdims=True)
        m_ref[hh] = m.astype(F32)
        probs.append(jnp.exp(diag[hh] - m))
    for hh in heads:
        acc_ref[hh] = _dot(probs[hh], va_ref[0, hh, i])
    for hh in heads:
        store_logits(0, 0, hh)

    @pl.loop(0, (i + g - 1) // g)
    def _(jj):
        slot = jj % 2
        for hh in range(n_h):
            s = [s_ref[slot, hh, t] for t in range(g)]
            m = m_ref[hh]
            s_max = functools.reduce(jnp.maximum, s)
            m_new = jnp.maximum(m, jnp.max(s_max, axis=1, keepdims=True).astype(F32))
            m_b = m_new.astype(BF16)
            p = jnp.concatenate([jnp.exp(s[t] - m_b) for t in range(g)], axis=1)
            v = va_ref[0, hh, pl.ds(g * jj, g)].reshape(g * tq, LANES)
            acc_ref[hh] = acc_ref[hh] * jnp.exp(m - m_new) + _dot(p, v)
            m_ref[hh] = m_new
        nxt = jnp.minimum(jj + 1, last_group)
        for hh in range(n_h):
            store_logits(1 - slot, nxt, hh)

    outs = [acc_ref[hh] / acc_ref[hh][:, HEAD_DIM:HEAD_DIM + 1] for hh in range(n_h)]
    for p in range(n_h // 2):
        o_ref[0, :, p * LANES:(p + 1) * LANES] = jnp.where(
            lane < HEAD_DIM, outs[2 * p], pltpu.roll(outs[2 * p + 1], HEAD_DIM, 1)).astype(BF16)


def _attn_prompt(q, kt, va, kmt, cache_kt, page_table, dq4, *, layer):
    b, n_heads, s, _ = q.shape
    nb = kt.shape[2]
    tq = MOBA_BLOCK
    hg = ATTN_HEADS_PER_STEP
    assert nb % ATTN_BLOCKS_PER_TRIP == 0 and n_heads % hg == 0 and hg % 2 == 0
    n_hq = n_heads // hg
    n_steps = b * n_hq * nb

    page_size = cache_kt.shape[-1]
    db, n_pages = page_table.shape
    ppb = MOBA_BLOCK // page_size
    assert MOBA_BLOCK % page_size == 0 and n_pages % ppb == 0 and n_pages // ppb <= LANES
    n_step = next(p for p in range(ppb, n_pages + 1, ppb)
                  if n_pages % p == 0 and db * (n_pages // p) <= n_steps)
    n_groups = db * (n_pages // n_step)

    def group(bi, hp, i):
        return jnp.minimum((bi * n_hq + hp) * nb + i, n_groups - 1)

    def page_spec(r):
        def index(bi, hp, i, pt):
            return (layer, pt[group(bi, hp, i) * n_step + r], 0, 0, 0)
        return pl.BlockSpec((1, 1, n_heads, HEAD_DIM, page_size), index)

    seq_of = lambda bi, hp, i: lax.div(group(bi, hp, i), n_pages // n_step)
    decode = dict(n_step=n_step, ppb=ppb, n_pages=n_pages, n_groups=n_groups)
    return pl.pallas_call(
        functools.partial(_attn_kernel, decode=decode),
        grid_spec=pltpu.PrefetchScalarGridSpec(
            num_scalar_prefetch=1,
            grid=(b, n_hq, nb),
            in_specs=[
                pl.BlockSpec((1, hg, tq, LANES), lambda bi, hp, i, pt: (bi, hp, i, 0)),
                pl.BlockSpec((1, hg, nb, LANES, tq), lambda bi, hp, i, pt: (bi, hp, 0, 0, 0),
                             pipeline_mode=pl.Buffered(1)),
                pl.BlockSpec((1, hg, nb, tq, LANES), lambda bi, hp, i, pt: (bi, hp, 0, 0, 0),
                             pipeline_mode=pl.Buffered(1)),
                pl.BlockSpec((1, hg, LANES, LANES), lambda bi, hp, i, pt: (bi, hp, 0, 0),
                             pipeline_mode=pl.Buffered(1)),
            ] + [page_spec(r) for r in range(n_step)] + [
                pl.BlockSpec((1, n_heads, HEAD_DIM, 1),
                             lambda bi, hp, i, pt: (seq_of(bi, hp, i), 0, 0, 0)),
            ],
            out_specs=[
                pl.BlockSpec((1, tq, hg * HEAD_DIM), lambda bi, hp, i, pt: (bi, i, hp)),
                pl.BlockSpec((1, n_heads, LANES), lambda bi, hp, i, pt: (seq_of(bi, hp, i), 0, 0)),
            ],
            scratch_shapes=[
                pltpu.VMEM((hg, tq, LANES), BF16),
                pltpu.VMEM((hg, tq, 1), F32),
                pltpu.VMEM((hg, tq, LANES), F32),
                pltpu.VMEM((2, hg, ATTN_BLOCKS_PER_TRIP, tq, MOBA_BLOCK), BF16),
            ],
        ),
        out_shape=[jax.ShapeDtypeStruct((b, s, n_heads * HEAD_DIM), BF16),
                   jax.ShapeDtypeStruct((db, n_heads, LANES), F32)],
        compiler_params=pltpu.CompilerParams(
            dimension_semantics=("arbitrary", "arbitrary", "arbitrary"),
            vmem_limit_bytes=ATTN_VMEM_LIMIT),
        name="attn_prompt",
    )(page_table.reshape(-1), q, kt, va, kmt, *([cache_kt] * n_step), dq4)


def _gate_table(kmean, n_heads):
    b, nb = kmean.shape[:2]
    t = kmean.reshape(b, nb, n_heads, HEAD_DIM).transpose(0, 2, 1, 3)
    return jnp.pad(t, ((0, 0), (0, 0), (HEAD_DIM, LANES - HEAD_DIM - nb), (0, LANES - HEAD_DIM)))


def _sproj_kernel(x_ref, nm_ref, win_ref, cos_ref, sin_ref, wpool_ref, pscale_ref, convw_ref,
                  spool_ref, sconv_ref, q_ref, k_ref, v_ref, pc_ref, u_ref, z_ref,
                  *, wa, wb, wc, pos):
    h = _rms(x_ref[...], nm_ref[...]).astype(BF16)
    cos, sin = cos_ref[...], sin_ref[...]
    scale = HEAD_DIM ** -0.5
    for c in range(wa // LANES):
        sl = slice(c * LANES, (c + 1) * LANES)
        q_ref[:, sl] = _rope_chunk(_dot(h, win_ref[:, sl]), cos, sin) * scale
        k_ref[:, sl] = _rope_chunk(_dot(h, win_ref[:, wa + c * LANES: wa + (c + 1) * LANES]), cos, sin)
        v_ref[:, sl] = _dot(h, win_ref[:, 2 * wa + c * LANES: 2 * wa + (c + 1) * LANES])
    o = 3 * wa
    u = _dot(h, win_ref[:, o:o + wb])
    hc = _dot(h, win_ref[:, o + wb:o + wb + wc])
    bg = _dot(h, win_ref[:, o + wb + wc:o + wb + 2 * wc])
    cg = _dot(h, win_ref[:, o + wb + 2 * wc:o + wb + 3 * wc])

    sums, run = {}, u
    for r in range(1, max(POOL_WINDOWS)):
        run = run + spool_ref[:, POOL_STATE - r, :]
        if r + 1 in POOL_WINDOWS:
            sums[r + 1] = run
    posv = jnp.full((u.shape[0], 1), pos, jnp.int32)
    d = _pool_select(sums[2], sums[4], sums[8], sums[16], posv) - u
    p = _dot(d.astype(BF16), wpool_ref[...]) * pscale_ref[...]

    z = cg * hc
    conv = (z * convw_ref[2:3, :] + sconv_ref[:, 1, :] * convw_ref[1:2, :]
            + sconv_ref[:, 0, :] * convw_ref[0:1, :])
    u_ref[...] = u
    z_ref[...] = z
    pc_ref[:, :wb] = p.astype(BF16)
    pc_ref[:, wb:] = (bg * conv).astype(BF16)


def _proj_sample(x, nm, w_in, cos, sin, wpool_bd, pscale, conv_w, spool, sconv, *, n_heads, pos):
    m, _ = x.shape
    wa = n_heads * HEAD_DIM
    wb = wpool_bd.shape[0]
    wc = conv_w.shape[1]
    assert spool.shape[1] == POOL_STATE and sconv.shape[1] == CONV_K - 1
    f = lambda w, dt: jax.ShapeDtypeStruct((m, w), dt)
    return pl.pallas_call(
        functools.partial(_sproj_kernel, wa=wa, wb=wb, wc=wc, pos=pos),
        out_shape=(f(wa, F32), f(wa, F32), f(wa, F32), f(wb + wc, BF16), f(wb, F32), f(wc, F32)),
        compiler_params=pltpu.CompilerParams(vmem_limit_bytes=VMEM_LIMIT),
        name="proj_sample",
    )(x, nm, w_in, cos, sin, wpool_bd, pscale, conv_w, spool, sconv)


def _stopk_kernel(gate_ref, idx_ref, *, n_blocks):
    gate = gate_ref[...]
    lane = lax.broadcasted_iota(jnp.int32, gate.shape, 1)
    lane_f = lane.astype(F32)
    gt = jnp.where(lane < n_blocks, gate, -jnp.inf)
    picks = jnp.zeros(gate.shape, F32)
    for r in range(MOBA_TOPK):
        mx = jnp.max(gt, axis=1, keepdims=True)
        first = jnp.min(jnp.where(gt == mx, lane_f, float(LANES)), axis=1, keepdims=True)
        gt = jnp.where(lane_f == first, -jnp.inf, gt)
        picks = jnp.where(lane == r, first, picks)
    idx_ref[...] = picks.astype(jnp.int32)


def _sample_topk(gates, n_blocks):
    assert MOBA_TOPK <= n_blocks <= LANES
    flat = gates.reshape(-1, LANES)
    return pl.pallas_call(
        functools.partial(_stopk_kernel, n_blocks=n_blocks),
        out_shape=jax.ShapeDtypeStruct(flat.shape, jnp.int32),
        name="sample_topk",
    )(flat).reshape(gates.shape)


def _sattn_kernel(idx_ref, pt_ref, *refs, n_fetch):
    k_refs = refs[:2 * n_fetch]
    v_refs = refs[2 * n_fetch:4 * n_fetch]
    q_ref, kn_ref, vn_ref, o_ref = refs[4 * n_fetch:]
    for hh in range(2):
        q = q_ref[0, hh]
        s_self = jnp.sum(q * kn_ref[0, hh], axis=0, keepdims=True)
        scores = [jnp.sum(k_refs[hh * n_fetch + t][0, 0, 0] * q, axis=0, keepdims=True)
                  for t in range(n_fetch)]
        m = s_self
        for s in scores:
            m = jnp.maximum(m, jnp.max(s, axis=1, keepdims=True))
        p_self = jnp.exp(s_self - m)
        l = p_self
        acc = p_self * vn_ref[0, hh]
        for t, s in enumerate(scores):
            p = jnp.exp(s - m)
            l = l + jnp.sum(p, axis=1, keepdims=True)
            acc = acc + jnp.sum(v_refs[hh * n_fetch + t][0, 0, 0] * p, axis=1, keepdims=True)
        o_ref[0, hh] = acc / l


def _sample_attn(cache_kt, cache_vt, page_table, idx, q4, kn4, vn4, *, layer):
    _, n_pool, n_heads, _, page_size = cache_kt.shape
    b = page_table.shape[0]
    ppb = MOBA_BLOCK // page_size
    n_fetch = MOBA_TOPK * ppb

    def page_spec(hh, r, t):
        def index(bi, hp, idx_s, pt):
            head = 2 * hp + hh
            blk = idx_s[(bi * n_heads + head) * MOBA_TOPK + r]
            return (layer, pt[bi, blk * ppb + t], head, 0, 0)
        return pl.BlockSpec((1, 1, 1, HEAD_DIM, page_size), index)

    fetch = [page_spec(hh, r, t) for hh in range(2) for r in range(MOBA_TOPK) for t in range(ppb)]
    vec = pl.BlockSpec((1, 2, HEAD_DIM, 1), lambda bi, hp, idx_s, pt: (bi, hp, 0, 0))
    return pl.pallas_call(
        functools.partial(_sattn_kernel, n_fetch=n_fetch),
        grid_spec=pltpu.PrefetchScalarGridSpec(
            num_scalar_prefetch=2,
            grid=(b, n_heads // 2),
            in_specs=fetch + fetch + [vec, vec, vec],
            out_specs=vec,
        ),
        out_shape=jax.ShapeDtypeStruct((b, n_heads, HEAD_DIM, 1), F32),
        compiler_params=_params(("parallel", "arbitrary")),
        name="sample_attn",
    )(idx.reshape(-1), page_table, *([cache_kt] * len(fetch)), *([cache_vt] * len(fetch)),
      q4, kn4, vn4)


def kernel(x_prompt, x_sample, cache_k, cache_v, page_table, state_pool, state_conv, norm_ffn1, ffn1_gate, ffn1_up, ffn1_down, norm_mix, w_in, w_pool, pool_scale, conv_w, w_o, norm_ffn2, ffn2_gate, ffn2_up, ffn2_down, norm_final):
    b, s, d = x_prompt.shape
    db, t_new, _ = x_sample.shape
    depth, n_pool, page_size, n_heads, head_dim = cache_k.shape
    assert head_dim == HEAD_DIM and t_new == 1 and n_heads % 2 == 0
    past_len = page_table.shape[1] * page_size
    assert past_len % MOBA_BLOCK == 0 and past_len + 1 >= max(POOL_WINDOWS)
    wa = n_heads * HEAD_DIM
    tm_prompt = 512 if (b * s) % 512 == 0 else MOBA_BLOCK

    cos_p, sin_p = _rope_tables(jnp.arange(s, dtype=jnp.int32))
    cos_s, sin_s = _rope_tables(past_len + jnp.arange(1, dtype=jnp.int32))
    row = lambda a: a.reshape(1, -1)
    bf = lambda a: a.astype(BF16)

    cache_kt = cache_k.transpose(0, 1, 3, 4, 2)
    cache_vt = cache_v.transpose(0, 1, 3, 4, 2)

    xp = x_prompt.reshape(b * s, d)
    xs = x_sample.reshape(db, d)
    outs = {n: [] for n in ("ks", "vs", "pp", "ps", "cp", "cs")}
    kv_all = None
    for l in range(depth):
        n_groups = w_pool.shape[1]
        wpool_bd = bf(jax.scipy.linalg.block_diag(*[w_pool[l, g] for g in range(n_groups)]))
        f1 = (row(norm_ffn1[l]), bf(ffn1_gate[l]), bf(ffn1_up[l]), bf(ffn1_down[l]))
        f2 = (row(norm_ffn2[l]), bf(ffn2_gate[l]), bf(ffn2_up[l]), bf(ffn2_down[l]))
        mixer_w = (row(norm_mix[l]), bf(w_in[l]))
        mixer_tail = (wpool_bd, row(pool_scale[l]), conv_w[l])
        wo = bf(w_o[l])

        last = l == depth - 1
        final = row(norm_final) if last else None

        xp = _ffn(xp, *f1, tm=tm_prompt)
        kp, vp, q, kt, va, pc, kmean, ptail, ctail = _proj_prompt(
            xp.reshape(b, s, d), *mixer_w, cos_p, sin_p, *mixer_tail, n_heads=n_heads,
            layer=l, depth=depth, kv_all=kv_all)
        kv_all = (kp, vp)
        xs = _ffn(xs, *f1, tm=db)
        qs, ksn, vsn, pcs, us, zs = _proj_sample(
            xs, *mixer_w, cos_s, sin_s, *mixer_tail, state_pool[l], state_conv[l],
            n_heads=n_heads, pos=past_len)
        col = lambda a: a.reshape(db, n_heads, HEAD_DIM, 1)

        a, gates = _attn_prompt(q, kt, va, _gate_table(kmean, n_heads), cache_kt, page_table,
                                col(qs), layer=l)
        xp = _ffn(xp, *f2, mix=(a.reshape(b * s, wa), pc.reshape(b * s, -1), wo), final=final,
                  tm=tm_prompt)
        outs["pp"].append(ptail[:, POOL_CARRY - POOL_STATE:])
        outs["cp"].append(ctail[:, CONV_CARRY - (CONV_K - 1):])

        idx = _sample_topk(gates, past_len // MOBA_BLOCK)
        a_s = _sample_attn(cache_kt, cache_vt, page_table, idx[:, :, :MOBA_TOPK],
                           col(qs), col(ksn), col(vsn), layer=l)
        xs = _ffn(xs, *f2, mix=(bf(a_s.reshape(db, wa)), pcs, wo), final=final, tm=db)
        outs["ks"].append(ksn.reshape(db, 1, n_heads, HEAD_DIM))
        outs["vs"].append(vsn.reshape(db, 1, n_heads, HEAD_DIM))
        outs["ps"].append(jnp.concatenate([state_pool[l][:, 1:], us[:, None]], axis=1))
        outs["cs"].append(jnp.concatenate([state_conv[l][:, 1:], zs[:, None]], axis=1))

    st = lambda n: jnp.stack(outs[n])
    k_prompt, v_prompt = (a.transpose(0, 1, 4, 2, 3) for a in kv_all)
    return (xp.reshape(b, s, d), xs.reshape(db, 1, d), k_prompt, v_prompt, st("ks"), st("vs"),
            st("pp"), st("ps"), st("cp"), st("cs"))
```

```python
import functools

import jax
import jax.numpy as jnp
from jax import lax
from jax.experimental import pallas as pl
from jax.experimental.pallas import tpu as pltpu

F32 = jnp.float32
BF16 = jnp.bfloat16

HEAD_DIM = 64
LANES = 128
MXU_DIM = 256
MOBA_BLOCK = 256
MOBA_TOPK = 3
POOL_WINDOWS = (2, 4, 8, 16)
POOL_STATE = max(POOL_WINDOWS) - 1
POOL_CARRY = 16
CONV_K = 3
CONV_CARRY = 8
ROPE_THETA = 10000.0
RMS_EPS = 1e-6
NEG = -1e30
VMEM_LIMIT = 56 * 1024 * 1024
ATTN_VMEM_LIMIT = 62 * 1024 * 1024


def _dot(a, b):
    return jnp.dot(a, b, preferred_element_type=F32)


def _rms(x, g):
    return x * lax.rsqrt(jnp.mean(x * x, axis=-1, keepdims=True) + RMS_EPS) * g


def _resident(shape):
    return pl.BlockSpec(shape, lambda *_: (0,) * len(shape), pipeline_mode=pl.Buffered(1))


def _params(sem):
    return pltpu.CompilerParams(dimension_semantics=sem, vmem_limit_bytes=VMEM_LIMIT)


def _ffn_kernel(*refs, premix, final_norm, ff_chunks):
    it = iter(refs)
    x_ref = next(it)
    if premix:
        a_ref, pc_ref, wo_ref = next(it), next(it), next(it)
    n_ref, wg_ref, wu_ref, wd_ref = next(it), next(it), next(it), next(it)
    if final_norm:
        nf_ref = next(it)
    o_ref = next(it)

    x = x_ref[...]
    if premix:
        wa = a_ref.shape[-1]
        x = x + _dot(a_ref[...], wo_ref[:wa, :]) + _dot(pc_ref[...], wo_ref[wa:, :])
    h = _rms(x, n_ref[...]).astype(BF16)
    y = jnp.zeros_like(x)
    for lo, hi in ff_chunks:
        g = _dot(h, wg_ref[:, lo:hi])
        u = _dot(h, wu_ref[:, lo:hi])
        act = (g * jax.nn.sigmoid(g) * u).astype(BF16)
        y = y + _dot(act, wd_ref[lo:hi, :])
    x = x + 0.5 * y
    if final_norm:
        x = _rms(x, nf_ref[...])
    o_ref[...] = x


def _ffn(x, norm, wg, wu, wd, *, mix=None, final=None, tm):
    m, d = x.shape
    ff = wg.shape[1]
    half = -(-(ff // 2) // MXU_DIM) * MXU_DIM
    ff_chunks = ((0, half), (half, ff))
    row = lambda w: pl.BlockSpec((tm, w), lambda i: (i, 0))
    args, specs = [x], [row(d)]
    if mix is not None:
        a, pc, wo = mix
        args += [a, pc, wo]
        specs += [row(a.shape[1]), row(pc.shape[1]), _resident(wo.shape)]
    args += [norm, wg, wu, wd]
    specs += [_resident(norm.shape), _resident(wg.shape), _resident(wu.shape), _resident(wd.shape)]
    if final is not None:
        args.append(final)
        specs.append(_resident(final.shape))
    return pl.pallas_call(
        functools.partial(_ffn_kernel, premix=mix is not None, final_norm=final is not None,
                          ff_chunks=ff_chunks),
        grid=(m // tm,),
        in_specs=specs,
        out_specs=row(d),
        out_shape=jax.ShapeDtypeStruct((m, d), F32),
        compiler_params=_params(("parallel",)),
        name="ffn",
    )(*args)


def _rope_chunk(x, cos, sin_signed):
    lane = lax.broadcasted_iota(jnp.int32, x.shape, 1)
    first_half = (lane % HEAD_DIM) < (HEAD_DIM // 2)
    partner = jnp.where(first_half, pltpu.roll(x, LANES - HEAD_DIM // 2, 1),
                        pltpu.roll(x, HEAD_DIM // 2, 1))
    return x * cos + partner * sin_signed


def _pool_select(s2, s4, s8, s16, pos):
    lane = lax.broadcasted_iota(jnp.int32, s2.shape, 1)
    group = s2.shape[1] // len(POOL_WINDOWS)
    posf = (pos + 1).astype(F32)
    mean = lambda s, w: s / jnp.minimum(posf, float(w))
    return jnp.where(lane < group, mean(s2, 2),
                     jnp.where(lane < 2 * group, mean(s4, 4),
                               jnp.where(lane < 3 * group, mean(s8, 8), mean(s16, 16))))


def _rope_tables(pos):
    half = HEAD_DIM // 2
    inv = jnp.power(ROPE_THETA, -jnp.arange(half, dtype=F32) / half)
    ang = pos.astype(F32)[:, None] * inv[None, :]
    cos, sin = jnp.cos(ang), jnp.sin(ang)
    reps = LANES // HEAD_DIM
    return (jnp.tile(jnp.concatenate([cos, cos], axis=1), (1, reps)),
            jnp.tile(jnp.concatenate([-sin, sin], axis=1), (1, reps)))


def _proj_kernel(x_ref, nm_ref, win_ref, cos_ref, sin_ref, wpool_ref, pscale_ref, convw_ref,
                 *refs, wa, wb, wc):
    (k_ref, v_ref, qt_ref, ka_ref, vt_ref, pc_ref, km_ref, ptail_ref, ctail_ref,
     prevu_ref, prevz_ref) = refs[-11:]
    i = pl.program_id(1)
    tm = x_ref.shape[1]
    n_chunks = wa // LANES

    @pl.when(i == 0)
    def _():
        prevu_ref[...] = jnp.zeros_like(prevu_ref)
        prevz_ref[...] = jnp.zeros_like(prevz_ref)

    h = _rms(x_ref[0], nm_ref[...]).astype(BF16)
    cos, sin = cos_ref[...], sin_ref[...]
    lane = lax.broadcasted_iota(jnp.int32, (tm, LANES), 1)
    low = lane < HEAD_DIM
    block_onehot = (lane - HEAD_DIM == i).astype(F32)
    row_t = lax.broadcasted_iota(jnp.int32, (HEAD_DIM, tm), 0)
    zeros_t = jnp.zeros((HEAD_DIM, tm), BF16)
    ones_row_t = (row_t == 0).astype(BF16)
    scale = HEAD_DIM ** -0.5

    wide = {}
    for c in range(n_chunks):
        sl = slice(c * LANES, (c + 1) * LANES)
        if c % (MXU_DIM // LANES) == 0:
            cols = slice(c * LANES, c * LANES + MXU_DIM)
            wide = {name: _dot(h, win_ref[:, off + cols.start:off + cols.stop])
                    for name, off in (("q", 0), ("k", wa), ("v", 2 * wa))}
        part = slice((c * LANES) % MXU_DIM, (c * LANES) % MXU_DIM + LANES)
        q = _rope_chunk(wide["q"][:, part], cos, sin) * scale
        k = _rope_chunk(wide["k"][:, part], cos, sin)
        v = wide["v"][:, part]
        km_ref[0, 0, :, sl] = jnp.mean(k, axis=0, keepdims=True)
        q_t, k_t, v_t = q.T, k.T, v.T
        for hh in range(2):
            head = 2 * c + hh
            rows = slice(hh * HEAD_DIM, (hh + 1) * HEAD_DIM)
            k_ref[0, 0, head] = k_t[rows]
            v_ref[0, 0, head] = v_t[rows]
            qt_ref[0, head] = jnp.concatenate([q_t[rows].astype(BF16), zeros_t], axis=0)
            vt_ref[0, head, 0] = jnp.concatenate([v_t[rows].astype(BF16), ones_row_t], axis=0)
            kh = k if hh == 0 else pltpu.roll(k, HEAD_DIM, 1)
            ka_ref[0, head, 0] = jnp.where(low, kh, block_onehot).astype(BF16)

    o = 3 * wa
    u = _dot(h, win_ref[:, o:o + wb])
    hc = _dot(h, win_ref[:, o + wb:o + wb + wc])
    bg = _dot(h, win_ref[:, o + wb + wc:o + wb + 2 * wc])
    cg = _dot(h, win_ref[:, o + wb + 2 * wc:o + wb + 3 * wc])

    pos = i * tm + lax.broadcasted_iota(jnp.int32, (tm, 1), 0)
    e = jnp.concatenate([prevu_ref[...], u], axis=0)
    s2 = e + pltpu.roll(e, 1, 0)
    s4 = s2 + pltpu.roll(s2, 2, 0)
    s8 = s4 + pltpu.roll(s4, 4, 0)
    s16 = s8 + pltpu.roll(s8, 8, 0)
    t = slice(POOL_CARRY, POOL_CARRY + tm)
    d = _pool_select(s2[t], s4[t], s8[t], s16[t], pos) - u
    p = _dot(d.astype(BF16), wpool_ref[...]) * pscale_ref[...]
    prevu_ref[...] = u[tm - POOL_CARRY:]
    ptail_ref[0] = u[tm - POOL_CARRY:]

    z = cg * hc
    ze = jnp.concatenate([prevz_ref[...], z], axis=0)
    tz = slice(CONV_CARRY, CONV_CARRY + tm)
    conv = (z * convw_ref[2:3, :] + pltpu.roll(ze, 1, 0)[tz] * convw_ref[1:2, :]
            + pltpu.roll(ze, 2, 0)[tz] * convw_ref[0:1, :])
    prevz_ref[...] = z[tm - CONV_CARRY:]
    ctail_ref[0] = z[tm - CONV_CARRY:]
    pc_ref[0, :, :wb] = p.astype(BF16)
    pc_ref[0, :, wb:] = (bg * conv).astype(BF16)


def _proj_prompt(x, nm, w_in, cos, sin, wpool_bd, pscale, conv_w, *, n_heads, layer, depth,
                 kv_all=None):
    b, s, d = x.shape
    tm = MOBA_BLOCK
    nb = s // tm
    wa = n_heads * HEAD_DIM
    wb = wpool_bd.shape[0]
    wc = conv_w.shape[1]
    assert s % tm == 0 and nb <= LANES - HEAD_DIM and wa % MXU_DIM == 0
    assert w_in.shape[1] == 3 * wa + wb + 3 * wc
    seq = lambda w: pl.BlockSpec((1, tm, w), lambda bi, i: (bi, i, 0))
    tab = pl.BlockSpec((tm, LANES), lambda bi, i: (i, 0))
    per_b = lambda r, w: pl.BlockSpec((1, r, w), lambda bi, i: (bi, 0, 0))
    head_t = pl.BlockSpec((1, 1, n_heads, HEAD_DIM, tm), lambda bi, i: (layer, bi, 0, 0, i))
    out_shape = (
        jax.ShapeDtypeStruct((depth, b, n_heads, HEAD_DIM, s), F32),
        jax.ShapeDtypeStruct((depth, b, n_heads, HEAD_DIM, s), F32),
        jax.ShapeDtypeStruct((b, n_heads, LANES, s), BF16),
        jax.ShapeDtypeStruct((b, n_heads, nb, tm, LANES), BF16),
        jax.ShapeDtypeStruct((b, n_heads, nb, LANES, tm), BF16),
        jax.ShapeDtypeStruct((b, s, wb + wc), BF16),
        jax.ShapeDtypeStruct((b, nb, 1, wa), F32),
        jax.ShapeDtypeStruct((b, POOL_CARRY, wb), F32),
        jax.ShapeDtypeStruct((b, CONV_CARRY, wc), F32),
    )
    out_specs = (
        head_t, head_t,
        pl.BlockSpec((1, n_heads, LANES, tm), lambda bi, i: (bi, 0, 0, i)),
        pl.BlockSpec((1, n_heads, 1, tm, LANES), lambda bi, i: (bi, 0, i, 0, 0)),
        pl.BlockSpec((1, n_heads, 1, LANES, tm), lambda bi, i: (bi, 0, i, 0, 0)),
        seq(wb + wc), pl.BlockSpec((1, 1, 1, wa), lambda bi, i: (bi, i, 0, 0)),
        per_b(POOL_CARRY, wb), per_b(CONV_CARRY, wc),
    )
    args = [x, nm, w_in, cos, sin, wpool_bd, pscale, conv_w]
    in_specs = [seq(d), _resident(nm.shape), _resident(w_in.shape), tab, tab,
                _resident(wpool_bd.shape), _resident(pscale.shape), _resident(conv_w.shape)]
    aliases = {}
    if kv_all is not None:
        aliases = {len(args): 0, len(args) + 1: 1}
        args += list(kv_all)
        in_specs += [pl.BlockSpec(memory_space=pl.ANY)] * 2
    return pl.pallas_call(
        functools.partial(_proj_kernel, wa=wa, wb=wb, wc=wc),
        grid=(b, nb),
        in_specs=in_specs,
        out_specs=out_specs,
        out_shape=out_shape,
        scratch_shapes=[pltpu.VMEM((POOL_CARRY, wb), F32), pltpu.VMEM((CONV_CARRY, wc), F32)],
        input_output_aliases=aliases,
        compiler_params=_params(("parallel", "arbitrary")),
        name="proj_prompt",
    )(*args)


def _top_blocks_t(gate_t, n_valid):
    blk = lax.broadcasted_iota(jnp.int32, gate_t.shape, 0)
    blk_f = blk.astype(F32)
    g = jnp.where(blk < n_valid, gate_t, -jnp.inf)
    picked = jnp.zeros(gate_t.shape, jnp.bool_)
    for _ in range(MOBA_TOPK):
        m = jnp.max(g, axis=0, keepdims=True)
        first = jnp.min(jnp.where(g == m, blk_f, float(2 * LANES)), axis=0, keepdims=True)
        hit = (blk_f == first) & (m > -jnp.inf)
        picked = picked | hit
        g = jnp.where(hit, -jnp.inf, g)
    return jnp.where(picked, 0.0, NEG)


def _dot_nt(a, b):
    return lax.dot_general(a, b, (((1,), (1,)), ((), ())), preferred_element_type=F32)


def _logits(a, b):
    return jnp.dot(a, b, preferred_element_type=F32).astype(BF16)


ATTN_BLOCKS_PER_TRIP = 2
ATTN_HEADS_PER_STEP = 8


def _decode_gates_step(page_refs, dq_ref, gate_ref, lin, *, pages_per_block, n_pages, n_groups):
    n_step = len(page_refs)
    n_heads = dq_ref.shape[1]

    @pl.when(lin < n_groups)
    def _():
        first_page = lax.rem(lin * n_step, n_pages)
        first_block = lax.div(first_page, pages_per_block)
        lane = lax.broadcasted_iota(jnp.int32, (n_heads, LANES), 1)

        @pl.when(first_page == 0)
        def _():
            gate_ref[...] = jnp.zeros_like(gate_ref)

        q = jnp.broadcast_to(dq_ref[0], page_refs[0].shape[2:])
        gate = gate_ref[0]
        for r in range(n_step // pages_per_block):
            tot = page_refs[r * pages_per_block][0, 0]
            for t in range(1, pages_per_block):
                tot = tot + page_refs[r * pages_per_block + t][0, 0]
            per_token = jnp.sum(tot * q, axis=1)
            val = jnp.sum(per_token, axis=1, keepdims=True) / float(MOBA_BLOCK)
            gate = jnp.where(lane == first_block + r, val, gate)
        gate_ref[0] = gate


def _attn_kernel(pt_ref, q_ref, ka_ref, vt_ref, kmr_ref, *refs, decode):
    n_step = decode["n_step"]
    page_refs, dq_ref = refs[:n_step], refs[n_step]
    o_ref, gate_ref, qs_ref, m_ref, acc_ref, s_ref = refs[n_step + 1:]
    lin = (pl.program_id(0) * pl.num_programs(1) + pl.program_id(1)) * pl.num_programs(2) \
        + pl.program_id(2)
    _decode_gates_step(page_refs, dq_ref, gate_ref, lin, pages_per_block=decode["ppb"],
                       n_pages=decode["n_pages"], n_groups=decode["n_groups"])

    i = pl.program_id(2)
    tq = q_ref.shape[3]
    nb = ka_ref.shape[2]
    nbp = -(-nb // 8) * 8
    key = lax.broadcasted_iota(jnp.int32, (MOBA_BLOCK, tq), 0)
    qry = lax.broadcasted_iota(jnp.int32, (MOBA_BLOCK, tq), 1)
    n_h = q_ref.shape[1]
    g = ATTN_BLOCKS_PER_TRIP
    last_group = nb // g - 1

    def store_logits(slot, jj, hh):
        keys = ka_ref[0, hh, pl.ds(g * jj, g)].reshape(g * MOBA_BLOCK, LANES)
        s_ref[slot, hh] = _logits(keys, qs_ref[hh])

    heads = range(n_h)
    gates, diag, probs = [], [], []
    for hh in heads:
        km = kmr_ref[0, hh]
        km_hi = km.astype(BF16)
        km_lo = (km - km_hi.astype(F32)).astype(BF16)
        gates.append(_dot(km_hi, q_ref[0, hh]) + _dot(km_lo, q_ref[0, hh]))
    for hh in heads:
        diag.append(jnp.where(key <= qry, _logits(ka_ref[0, hh, i], q_ref[0, hh]), NEG))
    for hh in heads:
        pen_t = _top_blocks_t(gates[hh][HEAD_DIM:HEAD_DIM + nbp], i)
        pen_t = jnp.concatenate([jnp.zeros((HEAD_DIM, tq), F32), pen_t,
                                 jnp.zeros((LANES - HEAD_DIM - nbp, tq), F32)], axis=0)
        qs_ref[hh] = (q_ref[0, hh].astype(F32) + pen_t).astype(BF16)
    for hh in heads:
        m = jnp.max(diag[hh], axis=0, keepdims=True)
        m_ref[hh] = m.astype(F32)
        probs.append(jnp.exp(diag[hh] - m))
    for hh in heads:
        acc_ref[hh] = _dot(vt_ref[0, hh, i], probs[hh])
    for hh in heads:
        store_logits(0, 0, hh)

    n_trips = (i + g - 1) // g

    @pl.loop(0, (n_trips + 1) // 2)
    def _(tt):
        for slot in range(2):
            jj = 2 * tt + slot
            nxt = jnp.minimum(jj + 1, last_group)
            for hh in range(n_h):
                s = s_ref[slot, hh]
                m = m_ref[hh]
                m_new = jnp.maximum(m, jnp.max(s, axis=0, keepdims=True).astype(F32))
                p = jnp.exp(s - m_new.astype(BF16))
                v = jnp.concatenate([vt_ref[0, hh, g * jj + t] for t in range(g)], axis=1)
                acc_ref[hh] = acc_ref[hh] * jnp.exp(m - m_new) + _dot(v, p)
                m_ref[hh] = m_new
                store_logits(1 - slot, nxt, hh)

    for p in range(n_h // 2):
        pair = [acc_ref[hh][:HEAD_DIM] / acc_ref[hh][HEAD_DIM:HEAD_DIM + 1] for hh in (2 * p, 2 * p + 1)]
        o_ref[0, :, p * LANES:(p + 1) * LANES] = jnp.concatenate(pair, axis=0).T.astype(BF16)


def _attn_prompt(q, ka, vt, kmt, cache_kt, page_table, dq4, *, layer):
    b, n_heads, _, s = q.shape
    nb = ka.shape[2]
    tq = MOBA_BLOCK
    hg = ATTN_HEADS_PER_STEP
    assert nb % ATTN_BLOCKS_PER_TRIP == 0 and n_heads % hg == 0 and hg % 2 == 0
    n_hq = n_heads // hg
    n_steps = b * n_hq * nb

    page_size = cache_kt.shape[-1]
    db, n_pages = page_table.shape
    ppb = MOBA_BLOCK // page_size
    assert MOBA_BLOCK % page_size == 0 and n_pages % ppb == 0 and n_pages // ppb <= LANES
    n_step = next(p for p in range(ppb, n_pages + 1, ppb)
                  if n_pages % p == 0 and db * (n_pages // p) <= n_steps)
    n_groups = db * (n_pages // n_step)

    def group(bi, hp, i):
        return jnp.minimum((bi * n_hq + hp) * nb + i, n_groups - 1)

    def page_spec(r):
        def index(bi, hp, i, pt):
            return (layer, pt[group(bi, hp, i) * n_step + r], 0, 0, 0)
        return pl.BlockSpec((1, 1, n_heads, HEAD_DIM, page_size), index)

    seq_of = lambda bi, hp, i: lax.div(group(bi, hp, i), n_pages // n_step)
    decode = dict(n_step=n_step, ppb=ppb, n_pages=n_pages, n_groups=n_groups)
    return pl.pallas_call(
        functools.partial(_attn_kernel, decode=decode),
        grid_spec=pltpu.PrefetchScalarGridSpec(
            num_scalar_prefetch=1,
            grid=(b, n_hq, nb),
            in_specs=[
                pl.BlockSpec((1, hg, LANES, tq), lambda bi, hp, i, pt: (bi, hp, 0, i)),
                pl.BlockSpec((1, hg, nb, tq, LANES), lambda bi, hp, i, pt: (bi, hp, 0, 0, 0),
                             pipeline_mode=pl.Buffered(1)),
                pl.BlockSpec((1, hg, nb, LANES, tq), lambda bi, hp, i, pt: (bi, hp, 0, 0, 0),
                             pipeline_mode=pl.Buffered(1)),
                pl.BlockSpec((1, hg, LANES, LANES), lambda bi, hp, i, pt: (bi, hp, 0, 0),
                             pipeline_mode=pl.Buffered(1)),
            ] + [page_spec(r) for r in range(n_step)] + [
                pl.BlockSpec((1, n_heads, HEAD_DIM, 1),
                             lambda bi, hp, i, pt: (seq_of(bi, hp, i), 0, 0, 0)),
            ],
            out_specs=[
                pl.BlockSpec((1, tq, hg * HEAD_DIM), lambda bi, hp, i, pt: (bi, i, hp)),
                pl.BlockSpec((1, n_heads, LANES), lambda bi, hp, i, pt: (seq_of(bi, hp, i), 0, 0)),
            ],
            scratch_shapes=[
                pltpu.VMEM((hg, LANES, tq), BF16),
                pltpu.VMEM((hg, 1, tq), F32),
                pltpu.VMEM((hg, LANES, tq), F32),
                pltpu.VMEM((2, hg, ATTN_BLOCKS_PER_TRIP * MOBA_BLOCK, tq), BF16),
            ],
        ),
        out_shape=[jax.ShapeDtypeStruct((b, s, n_heads * HEAD_DIM), BF16),
                   jax.ShapeDtypeStruct((db, n_heads, LANES), F32)],
        compiler_params=pltpu.CompilerParams(
            dimension_semantics=("arbitrary", "arbitrary", "arbitrary"),
            vmem_limit_bytes=ATTN_VMEM_LIMIT),
        name="attn_prompt",
    )(page_table.reshape(-1), q, ka, vt, kmt, *([cache_kt] * n_step), dq4)


def _gate_table(kmean, n_heads):
    b, nb = kmean.shape[:2]
    t = kmean.reshape(b, nb, n_heads, HEAD_DIM).transpose(0, 2, 1, 3)
    return jnp.pad(t, ((0, 0), (0, 0), (HEAD_DIM, LANES - HEAD_DIM - nb), (0, LANES - HEAD_DIM)))


def _sproj_kernel(x_ref, nm_ref, win_ref, cos_ref, sin_ref, wpool_ref, pscale_ref, convw_ref,
                  spool_ref, sconv_ref, q_ref, k_ref, v_ref, pc_ref, u_ref, z_ref,
                  *, wa, wb, wc, pos):
    h = _rms(x_ref[...], nm_ref[...]).astype(BF16)
    cos, sin = cos_ref[...], sin_ref[...]
    scale = HEAD_DIM ** -0.5
    for c in range(wa // LANES):
        sl = slice(c * LANES, (c + 1) * LANES)
        q_ref[:, sl] = _rope_chunk(_dot(h, win_ref[:, sl]), cos, sin) * scale
        k_ref[:, sl] = _rope_chunk(_dot(h, win_ref[:, wa + c * LANES: wa + (c + 1) * LANES]), cos, sin)
        v_ref[:, sl] = _dot(h, win_ref[:, 2 * wa + c * LANES: 2 * wa + (c + 1) * LANES])
    o = 3 * wa
    u = _dot(h, win_ref[:, o:o + wb])
    hc = _dot(h, win_ref[:, o + wb:o + wb + wc])
    bg = _dot(h, win_ref[:, o + wb + wc:o + wb + 2 * wc])
    cg = _dot(h, win_ref[:, o + wb + 2 * wc:o + wb + 3 * wc])

    sums, run = {}, u
    for r in range(1, max(POOL_WINDOWS)):
        run = run + spool_ref[:, POOL_STATE - r, :]
        if r + 1 in POOL_WINDOWS:
            sums[r + 1] = run
    posv = jnp.full((u.shape[0], 1), pos, jnp.int32)
    d = _pool_select(sums[2], sums[4], sums[8], sums[16], posv) - u
    p = _dot(d.astype(BF16), wpool_ref[...]) * pscale_ref[...]

    z = cg * hc
    conv = (z * convw_ref[2:3, :] + sconv_ref[:, 1, :] * convw_ref[1:2, :]
            + sconv_ref[:, 0, :] * convw_ref[0:1, :])
    u_ref[...] = u
    z_ref[...] = z
    pc_ref[:, :wb] = p.astype(BF16)
    pc_ref[:, wb:] = (bg * conv).astype(BF16)


def _proj_sample(x, nm, w_in, cos, sin, wpool_bd, pscale, conv_w, spool, sconv, *, n_heads, pos):
    m, _ = x.shape
    wa = n_heads * HEAD_DIM
    wb = wpool_bd.shape[0]
    wc = conv_w.shape[1]
    assert spool.shape[1] == POOL_STATE and sconv.shape[1] == CONV_K - 1
    f = lambda w, dt: jax.ShapeDtypeStruct((m, w), dt)
    return pl.pallas_call(
        functools.partial(_sproj_kernel, wa=wa, wb=wb, wc=wc, pos=pos),
        out_shape=(f(wa, F32), f(wa, F32), f(wa, F32), f(wb + wc, BF16), f(wb, F32), f(wc, F32)),
        compiler_params=pltpu.CompilerParams(vmem_limit_bytes=VMEM_LIMIT),
        name="proj_sample",
    )(x, nm, w_in, cos, sin, wpool_bd, pscale, conv_w, spool, sconv)


def _stopk_kernel(gate_ref, idx_ref, *, n_blocks):
    gate = gate_ref[...]
    lane = lax.broadcasted_iota(jnp.int32, gate.shape, 1)
    lane_f = lane.astype(F32)
    gt = jnp.where(lane < n_blocks, gate, -jnp.inf)
    picks = jnp.zeros(gate.shape, F32)
    for r in range(MOBA_TOPK):
        mx = jnp.max(gt, axis=1, keepdims=True)
        first = jnp.min(jnp.where(gt == mx, lane_f, float(LANES)), axis=1, keepdims=True)
        gt = jnp.where(lane_f == first, -jnp.inf, gt)
        picks = jnp.where(lane == r, first, picks)
    idx_ref[...] = picks.astype(jnp.int32)


def _sample_topk(gates, n_blocks):
    assert MOBA_TOPK <= n_blocks <= LANES
    flat = gates.reshape(-1, LANES)
    return pl.pallas_call(
        functools.partial(_stopk_kernel, n_blocks=n_blocks),
        out_shape=jax.ShapeDtypeStruct(flat.shape, jnp.int32),
        name="sample_topk",
    )(flat).reshape(gates.shape)


def _sattn_kernel(idx_ref, pt_ref, *refs, n_fetch):
    k_refs = refs[:2 * n_fetch]
    v_refs = refs[2 * n_fetch:4 * n_fetch]
    q_ref, kn_ref, vn_ref, o_ref = refs[4 * n_fetch:]
    for hh in range(2):
        q = q_ref[0, hh]
        s_self = jnp.sum(q * kn_ref[0, hh], axis=0, keepdims=True)
        scores = [jnp.sum(k_refs[hh * n_fetch + t][0, 0, 0] * q, axis=0, keepdims=True)
                  for t in range(n_fetch)]
        m = s_self
        for s in scores:
            m = jnp.maximum(m, jnp.max(s, axis=1, keepdims=True))
        p_self = jnp.exp(s_self - m)
        l = p_self
        acc = p_self * vn_ref[0, hh]
        for t, s in enumerate(scores):
            p = jnp.exp(s - m)
            l = l + jnp.sum(p, axis=1, keepdims=True)
            acc = acc + jnp.sum(v_refs[hh * n_fetch + t][0, 0, 0] * p, axis=1, keepdims=True)
        o_ref[0, hh] = acc / l


def _sample_attn(cache_kt, cache_vt, page_table, idx, q4, kn4, vn4, *, layer):
    _, n_pool, n_heads, _, page_size = cache_kt.shape
    b = page_table.shape[0]
    ppb = MOBA_BLOCK // page_size
    n_fetch = MOBA_TOPK * ppb

    def page_spec(hh, r, t):
        def index(bi, hp, idx_s, pt):
            head = 2 * hp + hh
            blk = idx_s[(bi * n_heads + head) * MOBA_TOPK + r]
            return (layer, pt[bi, blk * ppb + t], head, 0, 0)
        return pl.BlockSpec((1, 1, 1, HEAD_DIM, page_size), index)

    fetch = [page_spec(hh, r, t) for hh in range(2) for r in range(MOBA_TOPK) for t in range(ppb)]
    vec = pl.BlockSpec((1, 2, HEAD_DIM, 1), lambda bi, hp, idx_s, pt: (bi, hp, 0, 0))
    return pl.pallas_call(
        functools.partial(_sattn_kernel, n_fetch=n_fetch),
        grid_spec=pltpu.PrefetchScalarGridSpec(
            num_scalar_prefetch=2,
            grid=(b, n_heads // 2),
            in_specs=fetch + fetch + [vec, vec, vec],
            out_specs=vec,
        ),
        out_shape=jax.ShapeDtypeStruct((b, n_heads, HEAD_DIM, 1), F32),
        compiler_params=_params(("parallel", "arbitrary")),
        name="sample_attn",
    )(idx.reshape(-1), page_table, *([cache_kt] * len(fetch)), *([cache_vt] * len(fetch)),
      q4, kn4, vn4)


def kernel(x_prompt, x_sample, cache_k, cache_v, page_table, state_pool, state_conv, norm_ffn1, ffn1_gate, ffn1_up, ffn1_down, norm_mix, w_in, w_pool, pool_scale, conv_w, w_o, norm_ffn2, ffn2_gate, ffn2_up, ffn2_down, norm_final):
    b, s, d = x_prompt.shape
    db, t_new, _ = x_sample.shape
    depth, n_pool, page_size, n_heads, head_dim = cache_k.shape
    assert head_dim == HEAD_DIM and t_new == 1 and n_heads % 2 == 0
    past_len = page_table.shape[1] * page_size
    assert past_len % MOBA_BLOCK == 0 and past_len + 1 >= max(POOL_WINDOWS)
    wa = n_heads * HEAD_DIM
    tm_prompt = 512 if (b * s) % 512 == 0 else MOBA_BLOCK

    cos_p, sin_p = _rope_tables(jnp.arange(s, dtype=jnp.int32))
    cos_s, sin_s = _rope_tables(past_len + jnp.arange(1, dtype=jnp.int32))
    row = lambda a: a.reshape(1, -1)
    bf = lambda a: a.astype(BF16)

    cache_kt = cache_k.transpose(0, 1, 3, 4, 2)
    cache_vt = cache_v.transpose(0, 1, 3, 4, 2)

    xp = x_prompt.reshape(b * s, d)
    xs = x_sample.reshape(db, d)
    outs = {n: [] for n in ("ks", "vs", "pp", "ps", "cp", "cs")}
    kv_all = None
    for l in range(depth):
        n_groups = w_pool.shape[1]
        wpool_bd = bf(jax.scipy.linalg.block_diag(*[w_pool[l, g] for g in range(n_groups)]))
        f1 = (row(norm_ffn1[l]), bf(ffn1_gate[l]), bf(ffn1_up[l]), bf(ffn1_down[l]))
        f2 = (row(norm_ffn2[l]), bf(ffn2_gate[l]), bf(ffn2_up[l]), bf(ffn2_down[l]))
        mixer_w = (row(norm_mix[l]), bf(w_in[l]))
        mixer_tail = (wpool_bd, row(pool_scale[l]), conv_w[l])
        wo = bf(w_o[l])

        last = l == depth - 1
        final = row(norm_final) if last else None

        xp = _ffn(xp, *f1, tm=tm_prompt)
        kp, vp, q, kt, va, pc, kmean, ptail, ctail = _proj_prompt(
            xp.reshape(b, s, d), *mixer_w, cos_p, sin_p, *mixer_tail, n_heads=n_heads,
            layer=l, depth=depth, kv_all=kv_all)
        kv_all = (kp, vp)
        xs = _ffn(xs, *f1, tm=db)
        qs, ksn, vsn, pcs, us, zs = _proj_sample(
            xs, *mixer_w, cos_s, sin_s, *mixer_tail, state_pool[l], state_conv[l],
            n_heads=n_heads, pos=past_len)
        col = lambda a: a.reshape(db, n_heads, HEAD_DIM, 1)

        a, gates = _attn_prompt(q, kt, va, _gate_table(kmean, n_heads), cache_kt, page_table,
                                col(qs), layer=l)
        xp = _ffn(xp, *f2, mix=(a.reshape(b * s, wa), pc.reshape(b * s, -1), wo), final=final,
                  tm=tm_prompt)
        outs["pp"].append(ptail[:, POOL_CARRY - POOL_STATE:])
        outs["cp"].append(ctail[:, CONV_CARRY - (CONV_K - 1):])

        idx = _sample_topk(gates, past_len // MOBA_BLOCK)
        a_s = _sample_attn(cache_kt, cache_vt, page_table, idx[:, :, :MOBA_TOPK],
                           col(qs), col(ksn), col(vsn), layer=l)
        xs = _ffn(xs, *f2, mix=(bf(a_s.reshape(db, wa)), pcs, wo), final=final, tm=db)
        outs["ks"].append(ksn.reshape(db, 1, n_heads, HEAD_DIM))
        outs["vs"].append(vsn.reshape(db, 1, n_heads, HEAD_DIM))
        outs["ps"].append(jnp.concatenate([state_pool[l][:, 1:], us[:, None]], axis=1))
        outs["cs"].append(jnp.concatenate([state_conv[l][:, 1:], zs[:, None]], axis=1))

    st = lambda n: jnp.stack(outs[n])
    k_prompt, v_prompt = (a.transpose(0, 1, 4, 2, 3) for a in kv_all)
    return (xp.reshape(b, s, d), xs.reshape(db, 1, d), k_prompt, v_prompt, st("ks"), st("vs"),
            st("pp"), st("ps"), st("cp"), st("cs"))
```

```python
import functools

import jax
import jax.numpy as jnp
from jax import lax
from jax.experimental import pallas as pl
from jax.experimental.pallas import tpu as pltpu

F32 = jnp.float32
BF16 = jnp.bfloat16

HEAD_DIM = 64
LANES = 128
MXU_DIM = 256
MOBA_BLOCK = 256
MOBA_TOPK = 3
POOL_WINDOWS = (2, 4, 8, 16)
POOL_STATE = max(POOL_WINDOWS) - 1
POOL_CARRY = 16
CONV_K = 3
CONV_CARRY = 8
ROPE_THETA = 10000.0
RMS_EPS = 1e-6
NEG = -1e30
VMEM_LIMIT = 56 * 1024 * 1024
ATTN_VMEM_LIMIT = 62 * 1024 * 1024


def _dot(a, b):
    return jnp.dot(a, b, preferred_element_type=F32)


def _rms(x, g):
    return x * lax.rsqrt(jnp.mean(x * x, axis=-1, keepdims=True) + RMS_EPS) * g


def _resident(shape):
    return pl.BlockSpec(shape, lambda *_: (0,) * len(shape), pipeline_mode=pl.Buffered(1))


def _params(sem):
    return pltpu.CompilerParams(dimension_semantics=sem, vmem_limit_bytes=VMEM_LIMIT)


def _ffn_kernel(*refs, premix, final_norm, ff_chunks):
    it = iter(refs)
    x_ref = next(it)
    if premix:
        a_ref, pc_ref, wo_ref = next(it), next(it), next(it)
    n_ref, wg_ref, wu_ref, wd_ref = next(it), next(it), next(it), next(it)
    if final_norm:
        nf_ref = next(it)
    o_ref = next(it)

    x = x_ref[...]
    if premix:
        wa = a_ref.shape[-1]
        x = x + _dot(a_ref[...], wo_ref[:wa, :]) + _dot(pc_ref[...], wo_ref[wa:, :])
    h = _rms(x, n_ref[...]).astype(BF16)
    y = jnp.zeros_like(x)
    for lo, hi in ff_chunks:
        g = _dot(h, wg_ref[:, lo:hi])
        u = _dot(h, wu_ref[:, lo:hi])
        act = (g * jax.nn.sigmoid(g) * u).astype(BF16)
        y = y + _dot(act, wd_ref[lo:hi, :])
    x = x + 0.5 * y
    if final_norm:
        x = _rms(x, nf_ref[...])
    o_ref[...] = x


def _ffn(x, norm, wg, wu, wd, *, mix=None, final=None, tm):
    m, d = x.shape
    ff = wg.shape[1]
    half = -(-(ff // 2) // MXU_DIM) * MXU_DIM
    ff_chunks = ((0, half), (half, ff))
    row = lambda w: pl.BlockSpec((tm, w), lambda i: (i, 0))
    args, specs = [x], [row(d)]
    if mix is not None:
        a, pc, wo = mix
        args += [a, pc, wo]
        specs += [row(a.shape[1]), row(pc.shape[1]), _resident(wo.shape)]
    args += [norm, wg, wu, wd]
    specs += [_resident(norm.shape), _resident(wg.shape), _resident(wu.shape), _resident(wd.shape)]
    if final is not None:
        args.append(final)
        specs.append(_resident(final.shape))
    return pl.pallas_call(
        functools.partial(_ffn_kernel, premix=mix is not None, final_norm=final is not None,
                          ff_chunks=ff_chunks),
        grid=(m // tm,),
        in_specs=specs,
        out_specs=row(d),
        out_shape=jax.ShapeDtypeStruct((m, d), F32),
        compiler_params=_params(("parallel",)),
        name="ffn",
    )(*args)


def _rope_chunk(x, cos, sin_signed):
    lane = lax.broadcasted_iota(jnp.int32, x.shape, 1)
    first_half = (lane % HEAD_DIM) < (HEAD_DIM // 2)
    partner = jnp.where(first_half, pltpu.roll(x, LANES - HEAD_DIM // 2, 1),
                        pltpu.roll(x, HEAD_DIM // 2, 1))
    return x * cos + partner * sin_signed


def _pool_select(s2, s4, s8, s16, pos):
    lane = lax.broadcasted_iota(jnp.int32, s2.shape, 1)
    group = s2.shape[1] // len(POOL_WINDOWS)
    posf = (pos + 1).astype(F32)
    mean = lambda s, w: s / jnp.minimum(posf, float(w))
    return jnp.where(lane < group, mean(s2, 2),
                     jnp.where(lane < 2 * group, mean(s4, 4),
                               jnp.where(lane < 3 * group, mean(s8, 8), mean(s16, 16))))


def _rope_tables(pos):
    half = HEAD_DIM // 2
    inv = jnp.power(ROPE_THETA, -jnp.arange(half, dtype=F32) / half)
    ang = pos.astype(F32)[:, None] * inv[None, :]
    cos, sin = jnp.cos(ang), jnp.sin(ang)
    reps = LANES // HEAD_DIM
    return (jnp.tile(jnp.concatenate([cos, cos], axis=1), (1, reps)),
            jnp.tile(jnp.concatenate([-sin, sin], axis=1), (1, reps)))


def _proj_kernel(x_ref, nm_ref, win_ref, cos_ref, sin_ref, wpool_ref, pscale_ref, convw_ref,
                 *refs, wa, wb, wc):
    (k_ref, v_ref, qt_ref, ka_ref, vt_ref, pc_ref, km_ref, ptail_ref, ctail_ref,
     prevu_ref, prevz_ref) = refs[-11:]
    i = pl.program_id(1)
    tm = x_ref.shape[1]
    n_chunks = wa // LANES

    @pl.when(i == 0)
    def _():
        prevu_ref[...] = jnp.zeros_like(prevu_ref)
        prevz_ref[...] = jnp.zeros_like(prevz_ref)

    h = _rms(x_ref[0], nm_ref[...]).astype(BF16)
    cos, sin = cos_ref[...], sin_ref[...]
    lane = lax.broadcasted_iota(jnp.int32, (tm, LANES), 1)
    low = lane < HEAD_DIM
    block_onehot = (lane - HEAD_DIM == i).astype(F32)
    row_t = lax.broadcasted_iota(jnp.int32, (HEAD_DIM, tm), 0)
    zeros_t = jnp.zeros((HEAD_DIM, tm), BF16)
    ones_row_t = (row_t == 0).astype(BF16)
    scale = HEAD_DIM ** -0.5

    wide = {}
    for c in range(n_chunks):
        sl = slice(c * LANES, (c + 1) * LANES)
        if c % (MXU_DIM // LANES) == 0:
            cols = slice(c * LANES, c * LANES + MXU_DIM)
            wide = {name: _dot(h, win_ref[:, off + cols.start:off + cols.stop])
                    for name, off in (("q", 0), ("k", wa), ("v", 2 * wa))}
        part = slice((c * LANES) % MXU_DIM, (c * LANES) % MXU_DIM + LANES)
        q = _rope_chunk(wide["q"][:, part], cos, sin) * scale
        k = _rope_chunk(wide["k"][:, part], cos, sin)
        v = wide["v"][:, part]
        km_ref[0, 0, :, sl] = jnp.mean(k, axis=0, keepdims=True)
        q_t, k_t, v_t = q.T, k.T, v.T
        for hh in range(2):
            head = 2 * c + hh
            rows = slice(hh * HEAD_DIM, (hh + 1) * HEAD_DIM)
            k_ref[0, 0, head] = k_t[rows]
            v_ref[0, 0, head] = v_t[rows]
            qt_ref[0, head] = jnp.concatenate([q_t[rows].astype(BF16), zeros_t], axis=0)
            vt_ref[0, head, 0] = jnp.concatenate([v_t[rows].astype(BF16), ones_row_t], axis=0)
            kh = k if hh == 0 else pltpu.roll(k, HEAD_DIM, 1)
            ka_ref[0, head, 0] = jnp.where(low, kh, block_onehot).astype(BF16)

    o = 3 * wa
    u = _dot(h, win_ref[:, o:o + wb])
    hc = _dot(h, win_ref[:, o + wb:o + wb + wc])
    bg = _dot(h, win_ref[:, o + wb + wc:o + wb + 2 * wc])
    cg = _dot(h, win_ref[:, o + wb + 2 * wc:o + wb + 3 * wc])

    pos = i * tm + lax.broadcasted_iota(jnp.int32, (tm, 1), 0)
    e = jnp.concatenate([prevu_ref[...], u], axis=0)
    s2 = e + pltpu.roll(e, 1, 0)
    s4 = s2 + pltpu.roll(s2, 2, 0)
    s8 = s4 + pltpu.roll(s4, 4, 0)
    s16 = s8 + pltpu.roll(s8, 8, 0)
    t = slice(POOL_CARRY, POOL_CARRY + tm)
    d = _pool_select(s2[t], s4[t], s8[t], s16[t], pos) - u
    p = _dot(d.astype(BF16), wpool_ref[...]) * pscale_ref[...]
    prevu_ref[...] = u[tm - POOL_CARRY:]
    ptail_ref[0] = u[tm - POOL_CARRY:]

    z = cg * hc
    ze = jnp.concatenate([prevz_ref[...], z], axis=0)
    tz = slice(CONV_CARRY, CONV_CARRY + tm)
    conv = (z * convw_ref[2:3, :] + pltpu.roll(ze, 1, 0)[tz] * convw_ref[1:2, :]
            + pltpu.roll(ze, 2, 0)[tz] * convw_ref[0:1, :])
    prevz_ref[...] = z[tm - CONV_CARRY:]
    ctail_ref[0] = z[tm - CONV_CARRY:]
    pc_ref[0, :, :wb] = p.astype(BF16)
    pc_ref[0, :, wb:] = (bg * conv).astype(BF16)


def _proj_prompt(x, nm, w_in, cos, sin, wpool_bd, pscale, conv_w, *, n_heads, layer, depth,
                 kv_all=None):
    b, s, d = x.shape
    tm = MOBA_BLOCK
    nb = s // tm
    wa = n_heads * HEAD_DIM
    wb = wpool_bd.shape[0]
    wc = conv_w.shape[1]
    assert s % tm == 0 and nb <= LANES - HEAD_DIM and wa % MXU_DIM == 0
    assert w_in.shape[1] == 3 * wa + wb + 3 * wc
    seq = lambda w: pl.BlockSpec((1, tm, w), lambda bi, i: (bi, i, 0))
    tab = pl.BlockSpec((tm, LANES), lambda bi, i: (i, 0))
    per_b = lambda r, w: pl.BlockSpec((1, r, w), lambda bi, i: (bi, 0, 0))
    head_t = pl.BlockSpec((1, 1, n_heads, HEAD_DIM, tm), lambda bi, i: (layer, bi, 0, 0, i))
    out_shape = (
        jax.ShapeDtypeStruct((depth, b, n_heads, HEAD_DIM, s), F32),
        jax.ShapeDtypeStruct((depth, b, n_heads, HEAD_DIM, s), F32),
        jax.ShapeDtypeStruct((b, n_heads, LANES, s), BF16),
        jax.ShapeDtypeStruct((b, n_heads, nb, tm, LANES), BF16),
        jax.ShapeDtypeStruct((b, n_heads, nb, LANES, tm), BF16),
        jax.ShapeDtypeStruct((b, s, wb + wc), BF16),
        jax.ShapeDtypeStruct((b, nb, 1, wa), F32),
        jax.ShapeDtypeStruct((b, POOL_CARRY, wb), F32),
        jax.ShapeDtypeStruct((b, CONV_CARRY, wc), F32),
    )
    out_specs = (
        head_t, head_t,
        pl.BlockSpec((1, n_heads, LANES, tm), lambda bi, i: (bi, 0, 0, i)),
        pl.BlockSpec((1, n_heads, 1, tm, LANES), lambda bi, i: (bi, 0, i, 0, 0)),
        pl.BlockSpec((1, n_heads, 1, LANES, tm), lambda bi, i: (bi, 0, i, 0, 0)),
        seq(wb + wc), pl.BlockSpec((1, 1, 1, wa), lambda bi, i: (bi, i, 0, 0)),
        per_b(POOL_CARRY, wb), per_b(CONV_CARRY, wc),
    )
    args = [x, nm, w_in, cos, sin, wpool_bd, pscale, conv_w]
    in_specs = [seq(d), _resident(nm.shape), _resident(w_in.shape), tab, tab,
                _resident(wpool_bd.shape), _resident(pscale.shape), _resident(conv_w.shape)]
    aliases = {}
    if kv_all is not None:
        aliases = {len(args): 0, len(args) + 1: 1}
        args += list(kv_all)
        in_specs += [pl.BlockSpec(memory_space=pl.ANY)] * 2
    return pl.pallas_call(
        functools.partial(_proj_kernel, wa=wa, wb=wb, wc=wc),
        grid=(b, nb),
        in_specs=in_specs,
        out_specs=out_specs,
        out_shape=out_shape,
        scratch_shapes=[pltpu.VMEM((POOL_CARRY, wb), F32), pltpu.VMEM((CONV_CARRY, wc), F32)],
        input_output_aliases=aliases,
        compiler_params=_params(("parallel", "arbitrary")),
        name="proj_prompt",
    )(*args)


def _top_blocks_t(gate_t, n_valid):
    blk = lax.broadcasted_iota(jnp.int32, gate_t.shape, 0)
    blk_f = blk.astype(F32)
    g = jnp.where(blk < n_valid, gate_t, -jnp.inf)
    picked = jnp.zeros(gate_t.shape, jnp.bool_)
    for _ in range(MOBA_TOPK):
        m = jnp.max(g, axis=0, keepdims=True)
        first = jnp.min(jnp.where(g == m, blk_f, float(2 * LANES)), axis=0, keepdims=True)
        hit = (blk_f == first) & (m > -jnp.inf)
        picked = picked | hit
        g = jnp.where(hit, -jnp.inf, g)
    return jnp.where(picked, 0.0, NEG)


def _dot_nt(a, b):
    return lax.dot_general(a, b, (((1,), (1,)), ((), ())), preferred_element_type=F32)


def _logits(a, b):
    return jnp.dot(a, b, preferred_element_type=F32).astype(BF16)


ATTN_BLOCKS_PER_TRIP = 2
ATTN_HEADS_PER_STEP = 8


def _decode_gates_step(page_refs, dq_ref, gate_ref, gacc_ref, lin, *, pages_per_block, n_pages,
                       n_groups, n_grid_steps):
    n_step = len(page_refs)
    n_heads = dq_ref.shape[1]

    @pl.when(lin == 0)
    def _():
        gacc_ref[...] = jnp.zeros_like(gacc_ref)

    def accumulate():
        first_page = lax.rem(lin * n_step, n_pages)
        first_block = lax.div(first_page, pages_per_block)
        lane = lax.broadcasted_iota(jnp.int32, (n_heads, LANES), 1)
        q = jnp.broadcast_to(dq_ref[0], page_refs[0].shape[2:])
        gate = jnp.where(first_page == 0, 0.0, gacc_ref[...])
        for r in range(n_step // pages_per_block):
            tot = page_refs[r * pages_per_block][0, 0]
            for t in range(1, pages_per_block):
                tot = tot + page_refs[r * pages_per_block + t][0, 0]
            per_token = jnp.sum(tot * q, axis=1)
            val = jnp.sum(per_token, axis=1, keepdims=True) / float(MOBA_BLOCK)
            gate = jnp.where(lane == first_block + r, val, gate)
        gacc_ref[...] = gate
        gate_ref[0] = gate

    if n_groups == n_grid_steps:
        accumulate()
    else:
        pl.when(lin < n_groups)(accumulate)


def _attn_kernel(pt_ref, q_ref, ka_ref, vt_ref, kmr_ref, *refs, decode):
    n_step = decode["n_step"]
    page_refs, dq_ref = refs[:n_step], refs[n_step]
    o_ref, gate_ref, qs_ref, m_ref, acc_ref, s_ref, gacc_ref = refs[n_step + 1:]
    lin = (pl.program_id(0) * pl.num_programs(1) + pl.program_id(1)) * pl.num_programs(2) \
        + pl.program_id(2)
    _decode_gates_step(page_refs, dq_ref, gate_ref, gacc_ref, lin, pages_per_block=decode["ppb"],
                       n_pages=decode["n_pages"], n_groups=decode["n_groups"],
                       n_grid_steps=decode["n_grid_steps"])

    i = pl.program_id(2)
    tq = q_ref.shape[3]
    nb = ka_ref.shape[2]
    nbp = -(-nb // 8) * 8
    key = lax.broadcasted_iota(jnp.int32, (MOBA_BLOCK, tq), 0)
    qry = lax.broadcasted_iota(jnp.int32, (MOBA_BLOCK, tq), 1)
    n_h = q_ref.shape[1]
    g = ATTN_BLOCKS_PER_TRIP
    last_group = nb // g - 1

    def store_logits(slot, jj, hh):
        keys = ka_ref[0, hh, pl.ds(g * jj, g)].reshape(g * MOBA_BLOCK, LANES)
        s_ref[slot, hh] = _logits(keys, qs_ref[hh])

    heads = range(n_h)
    gates, diag, probs = [], [], []
    for hh in heads:
        km = kmr_ref[0, hh]
        km_hi = km.astype(BF16)
        km_lo = (km - km_hi.astype(F32)).astype(BF16)
        gates.append(_dot(km_hi, q_ref[0, hh]) + _dot(km_lo, q_ref[0, hh]))
    for hh in heads:
        diag.append(jnp.where(key <= qry, _logits(ka_ref[0, hh, i], q_ref[0, hh]), NEG))
    for hh in heads:
        pen_t = _top_blocks_t(gates[hh][HEAD_DIM:HEAD_DIM + nbp], i)
        pen_t = jnp.concatenate([jnp.zeros((HEAD_DIM, tq), F32), pen_t,
                                 jnp.zeros((LANES - HEAD_DIM - nbp, tq), F32)], axis=0)
        qs_ref[hh] = (q_ref[0, hh].astype(F32) + pen_t).astype(BF16)
    for hh in heads:
        m = jnp.max(diag[hh], axis=0, keepdims=True)
        m_ref[hh] = m.astype(F32)
        probs.append(jnp.exp(diag[hh] - m))
    for hh in heads:
        acc_ref[hh] = _dot(vt_ref[0, hh, i], probs[hh])
    for hh in heads:
        store_logits(0, 0, hh)

    n_trips = (i + g - 1) // g

    def trip(jj, slot, prefetch):
        nxt = jnp.minimum(jj + 1, last_group)
        for hh in range(n_h):
            s = s_ref[slot, hh]
            m = m_ref[hh]
            m_new = jnp.maximum(m, jnp.max(s, axis=0, keepdims=True).astype(F32))
            p = jnp.exp(s - m_new.astype(BF16))
            v = jnp.concatenate([vt_ref[0, hh, g * jj + t] for t in range(g)], axis=1)
            acc_ref[hh] = acc_ref[hh] * jnp.exp(m - m_new) + _dot(v, p)
            m_ref[hh] = m_new
            if prefetch:
                store_logits(1 - slot, nxt, hh)

    @pl.loop(0, n_trips // 2)
    def _(tt):
        trip(2 * tt, 0, True)
        trip(2 * tt + 1, 1, True)

    @pl.when(n_trips % 2 == 1)
    def _():
        trip(n_trips - 1, 0, False)

    for p in range(n_h // 2):
        pair = [acc_ref[hh][:HEAD_DIM] / acc_ref[hh][HEAD_DIM:HEAD_DIM + 1] for hh in (2 * p, 2 * p + 1)]
        o_ref[0, :, p * LANES:(p + 1) * LANES] = jnp.concatenate(pair, axis=0).T.astype(BF16)


def _attn_prompt(q, ka, vt, kmt, cache_kt, page_table, dq4, *, layer):
    b, n_heads, _, s = q.shape
    nb = ka.shape[2]
    tq = MOBA_BLOCK
    hg = ATTN_HEADS_PER_STEP
    assert nb % ATTN_BLOCKS_PER_TRIP == 0 and n_heads % hg == 0 and hg % 2 == 0
    n_hq = n_heads // hg
    n_steps = b * n_hq * nb

    page_size = cache_kt.shape[-1]
    db, n_pages = page_table.shape
    ppb = MOBA_BLOCK // page_size
    assert MOBA_BLOCK % page_size == 0 and n_pages % ppb == 0 and n_pages // ppb <= LANES
    n_step = next(p for p in range(ppb, n_pages + 1, ppb)
                  if n_pages % p == 0 and db * (n_pages // p) <= n_steps)
    n_groups = db * (n_pages // n_step)

    def group(bi, hp, i):
        return jnp.minimum((bi * n_hq + hp) * nb + i, n_groups - 1)

    def page_spec(r):
        def index(bi, hp, i, pt):
            return (layer, pt[group(bi, hp, i) * n_step + r], 0, 0, 0)
        return pl.BlockSpec((1, 1, n_heads, HEAD_DIM, page_size), index)

    seq_of = lambda bi, hp, i: lax.div(group(bi, hp, i), n_pages // n_step)
    decode = dict(n_step=n_step, ppb=ppb, n_pages=n_pages, n_groups=n_groups, n_grid_steps=n_steps)
    return pl.pallas_call(
        functools.partial(_attn_kernel, decode=decode),
        grid_spec=pltpu.PrefetchScalarGridSpec(
            num_scalar_prefetch=1,
            grid=(b, n_hq, nb),
            in_specs=[
                pl.BlockSpec((1, hg, LANES, tq), lambda bi, hp, i, pt: (bi, hp, 0, i)),
                pl.BlockSpec((1, hg, nb, tq, LANES), lambda bi, hp, i, pt: (bi, hp, 0, 0, 0),
                             pipeline_mode=pl.Buffered(1)),
                pl.BlockSpec((1, hg, nb, LANES, tq), lambda bi, hp, i, pt: (bi, hp, 0, 0, 0),
                             pipeline_mode=pl.Buffered(1)),
                pl.BlockSpec((1, hg, LANES, LANES), lambda bi, hp, i, pt: (bi, hp, 0, 0),
                             pipeline_mode=pl.Buffered(1)),
            ] + [page_spec(r) for r in range(n_step)] + [
                pl.BlockSpec((1, n_heads, HEAD_DIM, 1),
                             lambda bi, hp, i, pt: (seq_of(bi, hp, i), 0, 0, 0)),
            ],
            out_specs=[
                pl.BlockSpec((1, tq, hg * HEAD_DIM), lambda bi, hp, i, pt: (bi, i, hp)),
                pl.BlockSpec((1, n_heads, LANES), lambda bi, hp, i, pt: (seq_of(bi, hp, i), 0, 0)),
            ],
            scratch_shapes=[
                pltpu.VMEM((hg, LANES, tq), BF16),
                pltpu.VMEM((hg, 1, tq), F32),
                pltpu.VMEM((hg, LANES, tq), F32),
                pltpu.VMEM((2, hg, ATTN_BLOCKS_PER_TRIP * MOBA_BLOCK, tq), BF16),
                pltpu.VMEM((n_heads, LANES), F32),
            ],
        ),
        out_shape=[jax.ShapeDtypeStruct((b, s, n_heads * HEAD_DIM), BF16),
                   jax.ShapeDtypeStruct((db, n_heads, LANES), F32)],
        compiler_params=pltpu.CompilerParams(
            dimension_semantics=("arbitrary", "arbitrary", "arbitrary"),
            vmem_limit_bytes=ATTN_VMEM_LIMIT),
        name="attn_prompt",
    )(page_table.reshape(-1), q, ka, vt, kmt, *([cache_kt] * n_step), dq4)


def _gate_table(kmean, n_heads):
    b, nb = kmean.shape[:2]
    t = kmean.reshape(b, nb, n_heads, HEAD_DIM).transpose(0, 2, 1, 3)
    return jnp.pad(t, ((0, 0), (0, 0), (HEAD_DIM, LANES - HEAD_DIM - nb), (0, LANES - HEAD_DIM)))


def _sproj_kernel(x_ref, nm_ref, win_ref, cos_ref, sin_ref, wpool_ref, pscale_ref, convw_ref,
                  spool_ref, sconv_ref, q_ref, k_ref, v_ref, pc_ref, u_ref, z_ref,
                  *, wa, wb, wc, pos):
    h = _rms(x_ref[...], nm_ref[...]).astype(BF16)
    cos, sin = cos_ref[...], sin_ref[...]
    scale = HEAD_DIM ** -0.5
    for c in range(wa // LANES):
        sl = slice(c * LANES, (c + 1) * LANES)
        q_ref[:, sl] = _rope_chunk(_dot(h, win_ref[:, sl]), cos, sin) * scale
        k_ref[:, sl] = _rope_chunk(_dot(h, win_ref[:, wa + c * LANES: wa + (c + 1) * LANES]), cos, sin)
        v_ref[:, sl] = _dot(h, win_ref[:, 2 * wa + c * LANES: 2 * wa + (c + 1) * LANES])
    o = 3 * wa
    u = _dot(h, win_ref[:, o:o + wb])
    hc = _dot(h, win_ref[:, o + wb:o + wb + wc])
    bg = _dot(h, win_ref[:, o + wb + wc:o + wb + 2 * wc])
    cg = _dot(h, win_ref[:, o + wb + 2 * wc:o + wb + 3 * wc])

    sums, run = {}, u
    for r in range(1, max(POOL_WINDOWS)):
        run = run + spool_ref[:, POOL_STATE - r, :]
        if r + 1 in POOL_WINDOWS:
            sums[r + 1] = run
    posv = jnp.full((u.shape[0], 1), pos, jnp.int32)
    d = _pool_select(sums[2], sums[4], sums[8], sums[16], posv) - u
    p = _dot(d.astype(BF16), wpool_ref[...]) * pscale_ref[...]

    z = cg * hc
    conv = (z * convw_ref[2:3, :] + sconv_ref[:, 1, :] * convw_ref[1:2, :]
            + sconv_ref[:, 0, :] * convw_ref[0:1, :])
    u_ref[...] = u
    z_ref[...] = z
    pc_ref[:, :wb] = p.astype(BF16)
    pc_ref[:, wb:] = (bg * conv).astype(BF16)


def _proj_sample(x, nm, w_in, cos, sin, wpool_bd, pscale, conv_w, spool, sconv, *, n_heads, pos):
    m, _ = x.shape
    wa = n_heads * HEAD_DIM
    wb = wpool_bd.shape[0]
    wc = conv_w.shape[1]
    assert spool.shape[1] == POOL_STATE and sconv.shape[1] == CONV_K - 1
    f = lambda w, dt: jax.ShapeDtypeStruct((m, w), dt)
    return pl.pallas_call(
        functools.partial(_sproj_kernel, wa=wa, wb=wb, wc=wc, pos=pos),
        out_shape=(f(wa, F32), f(wa, F32), f(wa, F32), f(wb + wc, BF16), f(wb, F32), f(wc, F32)),
        compiler_params=pltpu.CompilerParams(vmem_limit_bytes=VMEM_LIMIT),
        name="proj_sample",
    )(x, nm, w_in, cos, sin, wpool_bd, pscale, conv_w, spool, sconv)


def _stopk_kernel(gate_ref, idx_ref, *, n_blocks):
    gate = gate_ref[...]
    lane = lax.broadcasted_iota(jnp.int32, gate.shape, 1)
    lane_f = lane.astype(F32)
    gt = jnp.where(lane < n_blocks, gate, -jnp.inf)
    picks = jnp.zeros(gate.shape, F32)
    for r in range(MOBA_TOPK):
        mx = jnp.max(gt, axis=1, keepdims=True)
        first = jnp.min(jnp.where(gt == mx, lane_f, float(LANES)), axis=1, keepdims=True)
        gt = jnp.where(lane_f == first, -jnp.inf, gt)
        picks = jnp.where(lane == r, first, picks)
    idx_ref[...] = picks.astype(jnp.int32)


def _sample_topk(gates, n_blocks):
    assert MOBA_TOPK <= n_blocks <= LANES
    flat = gates.reshape(-1, LANES)
    return pl.pallas_call(
        functools.partial(_stopk_kernel, n_blocks=n_blocks),
        out_shape=jax.ShapeDtypeStruct(flat.shape, jnp.int32),
        name="sample_topk",
    )(flat).reshape(gates.shape)


def _sattn_kernel(idx_ref, pt_ref, *refs, n_fetch):
    k_refs = refs[:2 * n_fetch]
    v_refs = refs[2 * n_fetch:4 * n_fetch]
    q_ref, kn_ref, vn_ref, o_ref = refs[4 * n_fetch:]
    for hh in range(2):
        q = q_ref[0, hh]
        s_self = jnp.sum(q * kn_ref[0, hh], axis=0, keepdims=True)
        scores = [jnp.sum(k_refs[hh * n_fetch + t][0, 0, 0] * q, axis=0, keepdims=True)
                  for t in range(n_fetch)]
        m = s_self
        for s in scores:
            m = jnp.maximum(m, jnp.max(s, axis=1, keepdims=True))
        p_self = jnp.exp(s_self - m)
        l = p_self
        acc = p_self * vn_ref[0, hh]
        for t, s in enumerate(scores):
            p = jnp.exp(s - m)
            l = l + jnp.sum(p, axis=1, keepdims=True)
            acc = acc + jnp.sum(v_refs[hh * n_fetch + t][0, 0, 0] * p, axis=1, keepdims=True)
        o_ref[0, hh] = acc / l


def _sample_attn(cache_kt, cache_vt, page_table, idx, q4, kn4, vn4, *, layer):
    _, n_pool, n_heads, _, page_size = cache_kt.shape
    b = page_table.shape[0]
    ppb = MOBA_BLOCK // page_size
    n_fetch = MOBA_TOPK * ppb

    def page_spec(hh, r, t):
        def index(bi, hp, idx_s, pt):
            head = 2 * hp + hh
            blk = idx_s[(bi * n_heads + head) * MOBA_TOPK + r]
            return (layer, pt[bi, blk * ppb + t], head, 0, 0)
        return pl.BlockSpec((1, 1, 1, HEAD_DIM, page_size), index)

    fetch = [page_spec(hh, r, t) for hh in range(2) for r in range(MOBA_TOPK) for t in range(ppb)]
    vec = pl.BlockSpec((1, 2, HEAD_DIM, 1), lambda bi, hp, idx_s, pt: (bi, hp, 0, 0))
    return pl.pallas_call(
        functools.partial(_sattn_kernel, n_fetch=n_fetch),
        grid_spec=pltpu.PrefetchScalarGridSpec(
            num_scalar_prefetch=2,
            grid=(b, n_heads // 2),
            in_specs=fetch + fetch + [vec, vec, vec],
            out_specs=vec,
        ),
        out_shape=jax.ShapeDtypeStruct((b, n_heads, HEAD_DIM, 1), F32),
        compiler_params=_params(("parallel", "arbitrary")),
        name="sample_attn",
    )(idx.reshape(-1), page_table, *([cache_kt] * len(fetch)), *([cache_vt] * len(fetch)),
      q4, kn4, vn4)


def kernel(x_prompt, x_sample, cache_k, cache_v, page_table, state_pool, state_conv, norm_ffn1, ffn1_gate, ffn1_up, ffn1_down, norm_mix, w_in, w_pool, pool_scale, conv_w, w_o, norm_ffn2, ffn2_gate, ffn2_up, ffn2_down, norm_final):
    b, s, d = x_prompt.shape
    db, t_new, _ = x_sample.shape
    depth, n_pool, page_size, n_heads, head_dim = cache_k.shape
    assert head_dim == HEAD_DIM and t_new == 1 and n_heads % 2 == 0
    past_len = page_table.shape[1] * page_size
    assert past_len % MOBA_BLOCK == 0 and past_len + 1 >= max(POOL_WINDOWS)
    wa = n_heads * HEAD_DIM
    tm_prompt = 512 if (b * s) % 512 == 0 else MOBA_BLOCK

    cos_p, sin_p = _rope_tables(jnp.arange(s, dtype=jnp.int32))
    cos_s, sin_s = _rope_tables(past_len + jnp.arange(1, dtype=jnp.int32))
    row = lambda a: a.reshape(1, -1)
    bf = lambda a: a.astype(BF16)

    cache_kt = cache_k.transpose(0, 1, 3, 4, 2)
    cache_vt = cache_v.transpose(0, 1, 3, 4, 2)

    xp = x_prompt.reshape(b * s, d)
    xs = x_sample.reshape(db, d)
    outs = {n: [] for n in ("ks", "vs", "pp", "ps", "cp", "cs")}
    kv_all = None
    for l in range(depth):
        n_groups = w_pool.shape[1]
        wpool_bd = bf(jax.scipy.linalg.block_diag(*[w_pool[l, g] for g in range(n_groups)]))
        f1 = (row(norm_ffn1[l]), bf(ffn1_gate[l]), bf(ffn1_up[l]), bf(ffn1_down[l]))
        f2 = (row(norm_ffn2[l]), bf(ffn2_gate[l]), bf(ffn2_up[l]), bf(ffn2_down[l]))
        mixer_w = (row(norm_mix[l]), bf(w_in[l]))
        mixer_tail = (wpool_bd, row(pool_scale[l]), conv_w[l])
        wo = bf(w_o[l])

        last = l == depth - 1
        final = row(norm_final) if last else None

        xp = _ffn(xp, *f1, tm=tm_prompt)
        kp, vp, q, kt, va, pc, kmean, ptail, ctail = _proj_prompt(
            xp.reshape(b, s, d), *mixer_w, cos_p, sin_p, *mixer_tail, n_heads=n_heads,
            layer=l, depth=depth, kv_all=kv_all)
        kv_all = (kp, vp)
        xs = _ffn(xs, *f1, tm=db)
        qs, ksn, vsn, pcs, us, zs = _proj_sample(
            xs, *mixer_w, cos_s, sin_s, *mixer_tail, state_pool[l], state_conv[l],
            n_heads=n_heads, pos=past_len)
        col = lambda a: a.reshape(db, n_heads, HEAD_DIM, 1)

        a, gates = _attn_prompt(q, kt, va, _gate_table(kmean, n_heads), cache_kt, page_table,
                                col(qs), layer=l)
        xp = _ffn(xp, *f2, mix=(a.reshape(b * s, wa), pc.reshape(b * s, -1), wo), final=final,
                  tm=tm_prompt)
        outs["pp"].append(ptail[:, POOL_CARRY - POOL_STATE:])
        outs["cp"].append(ctail[:, CONV_CARRY - (CONV_K - 1):])

        idx = _sample_topk(gates, past_len // MOBA_BLOCK)
        a_s = _sample_attn(cache_kt, cache_vt, page_table, idx[:, :, :MOBA_TOPK],
                           col(qs), col(ksn), col(vsn), layer=l)
        xs = _ffn(xs, *f2, mix=(bf(a_s.reshape(db, wa)), pcs, wo), final=final, tm=db)
        outs["ks"].append(ksn.reshape(db, 1, n_heads, HEAD_DIM))
        outs["vs"].append(vsn.reshape(db, 1, n_heads, HEAD_DIM))
        outs["ps"].append(jnp.concatenate([state_pool[l][:, 1:], us[:, None]], axis=1))
        outs["cs"].append(jnp.concatenate([state_conv[l][:, 1:], zs[:, None]], axis=1))

    st = lambda n: jnp.stack(outs[n])
    k_prompt, v_prompt = (a.transpose(0, 1, 4, 2, 3) for a in kv_all)
    return (xp.reshape(b, s, d), xs.reshape(db, 1, d), k_prompt, v_prompt, st("ks"), st("vs"),
            st("pp"), st("ps"), st("cp"), st("cs"))
```

```python
import functools

import jax
import jax.numpy as jnp
from jax import lax
from jax.experimental import pallas as pl
from jax.experimental.pallas import tpu as pltpu

F32 = jnp.float32
BF16 = jnp.bfloat16

HEAD_DIM = 64
LANES = 128
MXU_DIM = 256
MOBA_BLOCK = 256
MOBA_TOPK = 3
POOL_WINDOWS = (2, 4, 8, 16)
POOL_STATE = max(POOL_WINDOWS) - 1
POOL_CARRY = 16
CONV_K = 3
CONV_CARRY = 8
ROPE_THETA = 10000.0
RMS_EPS = 1e-6
NEG = -1e30
VMEM_LIMIT = 56 * 1024 * 1024
ATTN_VMEM_LIMIT = 62 * 1024 * 1024


def _dot(a, b):
    return jnp.dot(a, b, preferred_element_type=F32)


def _rms(x, g):
    return x * lax.rsqrt(jnp.mean(x * x, axis=-1, keepdims=True) + RMS_EPS) * g


def _resident(shape):
    return pl.BlockSpec(shape, lambda *_: (0,) * len(shape), pipeline_mode=pl.Buffered(1))


def _params(sem):
    return pltpu.CompilerParams(dimension_semantics=sem, vmem_limit_bytes=VMEM_LIMIT)


def _ffn_kernel(*refs, premix, final_norm, ff_chunks):
    it = iter(refs)
    x_ref = next(it)
    if premix:
        a_ref, pc_ref, wo_ref = next(it), next(it), next(it)
    n_ref, wg_ref, wu_ref, wd_ref = next(it), next(it), next(it), next(it)
    if final_norm:
        nf_ref = next(it)
    o_ref = next(it)

    x = x_ref[...]
    if premix:
        wa = a_ref.shape[-1]
        x = x + _dot(a_ref[...], wo_ref[:wa, :]) + _dot(pc_ref[...], wo_ref[wa:, :])
    h = _rms(x, n_ref[...]).astype(BF16)
    y = jnp.zeros_like(x)
    for lo, hi in ff_chunks:
        g = _dot(h, wg_ref[:, lo:hi])
        u = _dot(h, wu_ref[:, lo:hi])
        act = (g * jax.nn.sigmoid(g) * u).astype(BF16)
        y = y + _dot(act, wd_ref[lo:hi, :])
    x = x + 0.5 * y
    if final_norm:
        x = _rms(x, nf_ref[...])
    o_ref[...] = x


def _ffn(x, norm, wg, wu, wd, *, mix=None, final=None, tm):
    m, d = x.shape
    ff = wg.shape[1]
    half = -(-(ff // 2) // MXU_DIM) * MXU_DIM
    ff_chunks = ((0, half), (half, ff))
    row = lambda w: pl.BlockSpec((tm, w), lambda i: (i, 0))
    args, specs = [x], [row(d)]
    if mix is not None:
        a, pc, wo = mix
        args += [a, pc, wo]
        specs += [row(a.shape[1]), row(pc.shape[1]), _resident(wo.shape)]
    args += [norm, wg, wu, wd]
    specs += [_resident(norm.shape), _resident(wg.shape), _resident(wu.shape), _resident(wd.shape)]
    if final is not None:
        args.append(final)
        specs.append(_resident(final.shape))
    return pl.pallas_call(
        functools.partial(_ffn_kernel, premix=mix is not None, final_norm=final is not None,
                          ff_chunks=ff_chunks),
        grid=(m // tm,),
        in_specs=specs,
        out_specs=row(d),
        out_shape=jax.ShapeDtypeStruct((m, d), F32),
        compiler_params=_params(("parallel",)),
        name="ffn",
    )(*args)


def _rope_chunk(x, cos, sin_signed):
    lane = lax.broadcasted_iota(jnp.int32, x.shape, 1)
    first_half = (lane % HEAD_DIM) < (HEAD_DIM // 2)
    partner = jnp.where(first_half, pltpu.roll(x, LANES - HEAD_DIM // 2, 1),
                        pltpu.roll(x, HEAD_DIM // 2, 1))
    return x * cos + partner * sin_signed


def _pool_select(s2, s4, s8, s16, pos):
    lane = lax.broadcasted_iota(jnp.int32, s2.shape, 1)
    group = s2.shape[1] // len(POOL_WINDOWS)
    posf = (pos + 1).astype(F32)
    mean = lambda s, w: s / jnp.minimum(posf, float(w))
    return jnp.where(lane < group, mean(s2, 2),
                     jnp.where(lane < 2 * group, mean(s4, 4),
                               jnp.where(lane < 3 * group, mean(s8, 8), mean(s16, 16))))


def _rope_tables(pos):
    half = HEAD_DIM // 2
    inv = jnp.power(ROPE_THETA, -jnp.arange(half, dtype=F32) / half)
    ang = pos.astype(F32)[:, None] * inv[None, :]
    cos, sin = jnp.cos(ang), jnp.sin(ang)
    reps = LANES // HEAD_DIM
    return (jnp.tile(jnp.concatenate([cos, cos], axis=1), (1, reps)),
            jnp.tile(jnp.concatenate([-sin, sin], axis=1), (1, reps)))


def _proj_kernel(x_ref, nm_ref, win_ref, cos_ref, sin_ref, wpool_ref, pscale_ref, convw_ref,
                 *refs, wa, wb, wc):
    (k_ref, v_ref, qt_ref, ka_ref, vt_ref, pc_ref, km_ref, ptail_ref, ctail_ref,
     prevu_ref, prevz_ref) = refs[-11:]
    i = pl.program_id(1)
    tm = x_ref.shape[1]
    n_chunks = wa // LANES

    @pl.when(i == 0)
    def _():
        prevu_ref[...] = jnp.zeros_like(prevu_ref)
        prevz_ref[...] = jnp.zeros_like(prevz_ref)

    h = _rms(x_ref[0], nm_ref[...]).astype(BF16)
    cos, sin = cos_ref[...], sin_ref[...]
    lane = lax.broadcasted_iota(jnp.int32, (tm, LANES), 1)
    low = lane < HEAD_DIM
    block_onehot = (lane - HEAD_DIM == i).astype(F32)
    row_t = lax.broadcasted_iota(jnp.int32, (HEAD_DIM, tm), 0)
    zeros_t = jnp.zeros((HEAD_DIM, tm), BF16)
    ones_row_t = (row_t == 0).astype(BF16)
    scale = HEAD_DIM ** -0.5

    wide = {}
    for c in range(n_chunks):
        sl = slice(c * LANES, (c + 1) * LANES)
        if c % (MXU_DIM // LANES) == 0:
            cols = slice(c * LANES, c * LANES + MXU_DIM)
            wide = {name: _dot(h, win_ref[:, off + cols.start:off + cols.stop])
                    for name, off in (("q", 0), ("k", wa), ("v", 2 * wa))}
        part = slice((c * LANES) % MXU_DIM, (c * LANES) % MXU_DIM + LANES)
        q = _rope_chunk(wide["q"][:, part], cos, sin) * scale
        k = _rope_chunk(wide["k"][:, part], cos, sin)
        v = wide["v"][:, part]
        km_ref[0, 0, :, sl] = jnp.mean(k, axis=0, keepdims=True)
        q_t, k_t, v_t = q.T, k.T, v.T
        for hh in range(2):
            head = 2 * c + hh
            rows = slice(hh * HEAD_DIM, (hh + 1) * HEAD_DIM)
            k_ref[0, 0, head] = k_t[rows]
            v_ref[0, 0, head] = v_t[rows]
            qt_ref[0, head] = jnp.concatenate([q_t[rows].astype(BF16), zeros_t], axis=0)
            vt_ref[0, head, 0] = jnp.concatenate([v_t[rows].astype(BF16), ones_row_t], axis=0)
            kh = k if hh == 0 else pltpu.roll(k, HEAD_DIM, 1)
            ka_ref[0, head, 0] = jnp.where(low, kh, block_onehot).astype(BF16)

    o = 3 * wa
    u = _dot(h, win_ref[:, o:o + wb])
    hc = _dot(h, win_ref[:, o + wb:o + wb + wc])
    bg = _dot(h, win_ref[:, o + wb + wc:o + wb + 2 * wc])
    cg = _dot(h, win_ref[:, o + wb + 2 * wc:o + wb + 3 * wc])

    pos = i * tm + lax.broadcasted_iota(jnp.int32, (tm, 1), 0)
    e = jnp.concatenate([prevu_ref[...], u], axis=0)
    s2 = e + pltpu.roll(e, 1, 0)
    s4 = s2 + pltpu.roll(s2, 2, 0)
    s8 = s4 + pltpu.roll(s4, 4, 0)
    s16 = s8 + pltpu.roll(s8, 8, 0)
    t = slice(POOL_CARRY, POOL_CARRY + tm)
    d = _pool_select(s2[t], s4[t], s8[t], s16[t], pos) - u
    p = _dot(d.astype(BF16), wpool_ref[...]) * pscale_ref[...]
    prevu_ref[...] = u[tm - POOL_CARRY:]
    ptail_ref[0] = u[tm - POOL_CARRY:]

    z = cg * hc
    ze = jnp.concatenate([prevz_ref[...], z], axis=0)
    tz = slice(CONV_CARRY, CONV_CARRY + tm)
    conv = (z * convw_ref[2:3, :] + pltpu.roll(ze, 1, 0)[tz] * convw_ref[1:2, :]
            + pltpu.roll(ze, 2, 0)[tz] * convw_ref[0:1, :])
    prevz_ref[...] = z[tm - CONV_CARRY:]
    ctail_ref[0] = z[tm - CONV_CARRY:]
    pc_ref[0, :, :wb] = p.astype(BF16)
    pc_ref[0, :, wb:] = (bg * conv).astype(BF16)


def _proj_prompt(x, nm, w_in, cos, sin, wpool_bd, pscale, conv_w, *, n_heads, layer, depth,
                 kv_all=None):
    b, s, d = x.shape
    tm = MOBA_BLOCK
    nb = s // tm
    wa = n_heads * HEAD_DIM
    wb = wpool_bd.shape[0]
    wc = conv_w.shape[1]
    assert s % tm == 0 and nb <= LANES - HEAD_DIM and wa % MXU_DIM == 0
    assert w_in.shape[1] == 3 * wa + wb + 3 * wc
    seq = lambda w: pl.BlockSpec((1, tm, w), lambda bi, i: (bi, i, 0))
    tab = pl.BlockSpec((tm, LANES), lambda bi, i: (i, 0))
    per_b = lambda r, w: pl.BlockSpec((1, r, w), lambda bi, i: (bi, 0, 0))
    head_t = pl.BlockSpec((1, 1, n_heads, HEAD_DIM, tm), lambda bi, i: (layer, bi, 0, 0, i))
    out_shape = (
        jax.ShapeDtypeStruct((depth, b, n_heads, HEAD_DIM, s), F32),
        jax.ShapeDtypeStruct((depth, b, n_heads, HEAD_DIM, s), F32),
        jax.ShapeDtypeStruct((b, n_heads, LANES, s), BF16),
        jax.ShapeDtypeStruct((b, n_heads, nb, tm, LANES), BF16),
        jax.ShapeDtypeStruct((b, n_heads, nb, LANES, tm), BF16),
        jax.ShapeDtypeStruct((b, s, wb + wc), BF16),
        jax.ShapeDtypeStruct((b, nb, 1, wa), F32),
        jax.ShapeDtypeStruct((b, POOL_CARRY, wb), F32),
        jax.ShapeDtypeStruct((b, CONV_CARRY, wc), F32),
    )
    out_specs = (
        head_t, head_t,
        pl.BlockSpec((1, n_heads, LANES, tm), lambda bi, i: (bi, 0, 0, i)),
        pl.BlockSpec((1, n_heads, 1, tm, LANES), lambda bi, i: (bi, 0, i, 0, 0)),
        pl.BlockSpec((1, n_heads, 1, LANES, tm), lambda bi, i: (bi, 0, i, 0, 0)),
        seq(wb + wc), pl.BlockSpec((1, 1, 1, wa), lambda bi, i: (bi, i, 0, 0)),
        per_b(POOL_CARRY, wb), per_b(CONV_CARRY, wc),
    )
    args = [x, nm, w_in, cos, sin, wpool_bd, pscale, conv_w]
    in_specs = [seq(d), _resident(nm.shape), _resident(w_in.shape), tab, tab,
                _resident(wpool_bd.shape), _resident(pscale.shape), _resident(conv_w.shape)]
    aliases = {}
    if kv_all is not None:
        aliases = {len(args): 0, len(args) + 1: 1}
        args += list(kv_all)
        in_specs += [pl.BlockSpec(memory_space=pl.ANY)] * 2
    return pl.pallas_call(
        functools.partial(_proj_kernel, wa=wa, wb=wb, wc=wc),
        grid=(b, nb),
        in_specs=in_specs,
        out_specs=out_specs,
        out_shape=out_shape,
        scratch_shapes=[pltpu.VMEM((POOL_CARRY, wb), F32), pltpu.VMEM((CONV_CARRY, wc), F32)],
        input_output_aliases=aliases,
        compiler_params=_params(("parallel", "arbitrary")),
        name="proj_prompt",
    )(*args)


def _top_blocks_t(gate_t, n_valid):
    blk = lax.broadcasted_iota(jnp.int32, gate_t.shape, 0)
    blk_f = blk.astype(F32)
    g = jnp.where(blk < n_valid, gate_t, -jnp.inf)
    picked = jnp.zeros(gate_t.shape, jnp.bool_)
    for _ in range(MOBA_TOPK):
        m = jnp.max(g, axis=0, keepdims=True)
        first = jnp.min(jnp.where(g == m, blk_f, float(2 * LANES)), axis=0, keepdims=True)
        hit = (blk_f == first) & (m > -jnp.inf)
        picked = picked | hit
        g = jnp.where(hit, -jnp.inf, g)
    return jnp.where(picked, 0.0, NEG)


def _logits(a, b):
    return jnp.dot(a, b, preferred_element_type=F32).astype(BF16)


ATTN_BLOCKS_PER_TRIP = 2
ATTN_HEADS_PER_STEP = 8


def _decode_gates_step(page_refs, dq_ref, gate_ref, gacc_ref, lin, *, pages_per_block, n_pages,
                       n_groups, n_grid_steps):
    n_step = len(page_refs)
    n_heads = dq_ref.shape[1]

    @pl.when(lin == 0)
    def _():
        gacc_ref[...] = jnp.zeros_like(gacc_ref)

    def accumulate():
        first_page = lax.rem(lin * n_step, n_pages)
        first_block = lax.div(first_page, pages_per_block)
        lane = lax.broadcasted_iota(jnp.int32, (n_heads, LANES), 1)
        q = dq_ref[0]
        gate = jnp.where(first_page == 0, 0.0, gacc_ref[...])
        for r in range(n_step // pages_per_block):
            tot = page_refs[r * pages_per_block][0, 0]
            for t in range(1, pages_per_block):
                tot = tot + page_refs[r * pages_per_block + t][0, 0]
            k_sum = jnp.sum(tot, axis=2, keepdims=True)
            val = jnp.sum(k_sum * q, axis=1) / float(MOBA_BLOCK)
            gate = jnp.where(lane == first_block + r, val, gate)
        gacc_ref[...] = gate
        gate_ref[0] = gate

    if n_groups == n_grid_steps:
        accumulate()
    else:
        pl.when(lin < n_groups)(accumulate)


def _attn_kernel(pt_ref, q_ref, ka_ref, vt_ref, kmr_ref, *refs, decode):
    n_step = decode["n_step"]
    page_refs, dq_ref = refs[:n_step], refs[n_step]
    o_ref, gate_ref, qs_ref, m_ref, acc_ref, s_ref, gacc_ref = refs[n_step + 1:]
    lin = (pl.program_id(0) * pl.num_programs(1) + pl.program_id(1)) * pl.num_programs(2) \
        + pl.program_id(2)
    _decode_gates_step(page_refs, dq_ref, gate_ref, gacc_ref, lin, pages_per_block=decode["ppb"],
                       n_pages=decode["n_pages"], n_groups=decode["n_groups"],
                       n_grid_steps=decode["n_grid_steps"])

    i = pl.program_id(2)
    tq = q_ref.shape[3]
    nb = ka_ref.shape[2]
    nbp = -(-nb // 8) * 8
    key = lax.broadcasted_iota(jnp.int32, (MOBA_BLOCK, tq), 0)
    qry = lax.broadcasted_iota(jnp.int32, (MOBA_BLOCK, tq), 1)
    n_h = q_ref.shape[1]
    g = ATTN_BLOCKS_PER_TRIP
    last_group = nb // g - 1

    def store_logits(slot, jj, hh):
        keys = ka_ref[0, hh, pl.ds(g * jj, g)].reshape(g * MOBA_BLOCK, LANES)
        s_ref[slot, hh] = _logits(keys, qs_ref[hh])

    heads = range(n_h)
    gates, diag, probs = [], [], []
    for hh in heads:
        km = kmr_ref[0, hh, HEAD_DIM:HEAD_DIM + nbp]
        km_hi = km.astype(BF16)
        km_lo = (km - km_hi.astype(F32)).astype(BF16)
        gates.append(_dot(km_hi, q_ref[0, hh]) + _dot(km_lo, q_ref[0, hh]))
    for hh in heads:
        diag.append(jnp.where(key <= qry, _logits(ka_ref[0, hh, i], q_ref[0, hh]), NEG))
    for hh in heads:
        pen_t = _top_blocks_t(gates[hh], i)
        pen_t = jnp.concatenate([jnp.zeros((HEAD_DIM, tq), F32), pen_t,
                                 jnp.zeros((LANES - HEAD_DIM - nbp, tq), F32)], axis=0)
        qs_ref[hh] = (q_ref[0, hh].astype(F32) + pen_t).astype(BF16)
    for hh in heads:
        m = jnp.max(diag[hh], axis=0, keepdims=True)
        m_ref[hh] = m.astype(F32)
        probs.append(jnp.exp(diag[hh] - m))
    for hh in heads:
        acc_ref[hh] = _dot(vt_ref[0, hh, i], probs[hh])
    for hh in heads:
        store_logits(0, 0, hh)

    n_trips = (i + g - 1) // g

    def trip(jj, slot, prefetch):
        nxt = jnp.minimum(jj + 1, last_group)
        for hh in range(n_h):
            s = s_ref[slot, hh]
            m = m_ref[hh]
            m_new = jnp.maximum(m, jnp.max(s, axis=0, keepdims=True).astype(F32))
            p = jnp.exp(s - m_new.astype(BF16))
            v = jnp.concatenate([vt_ref[0, hh, g * jj + t] for t in range(g)], axis=1)
            acc_ref[hh] = acc_ref[hh] * jnp.exp(m - m_new) + _dot(v, p)
            m_ref[hh] = m_new
            if prefetch:
                store_logits(1 - slot, nxt, hh)

    @pl.loop(0, n_trips // 2)
    def _(tt):
        trip(2 * tt, 0, True)
        trip(2 * tt + 1, 1, True)

    @pl.when(n_trips % 2 == 1)
    def _():
        trip(n_trips - 1, 0, False)

    for p in range(n_h // 2):
        pair = [acc_ref[hh][:HEAD_DIM] / acc_ref[hh][HEAD_DIM:HEAD_DIM + 1] for hh in (2 * p, 2 * p + 1)]
        o_ref[0, :, p * LANES:(p + 1) * LANES] = jnp.concatenate(pair, axis=0).T.astype(BF16)


def _attn_prompt(q, ka, vt, kmt, cache_kt, page_table, dq4, *, layer):
    b, n_heads, _, s = q.shape
    nb = ka.shape[2]
    tq = MOBA_BLOCK
    hg = ATTN_HEADS_PER_STEP
    assert nb % ATTN_BLOCKS_PER_TRIP == 0 and n_heads % hg == 0 and hg % 2 == 0
    n_hq = n_heads // hg
    n_steps = b * n_hq * nb

    page_size = cache_kt.shape[-1]
    db, n_pages = page_table.shape
    ppb = MOBA_BLOCK // page_size
    assert MOBA_BLOCK % page_size == 0 and n_pages % ppb == 0 and n_pages // ppb <= LANES
    n_step = next(p for p in range(ppb, n_pages + 1, ppb)
                  if n_pages % p == 0 and db * (n_pages // p) <= n_steps)
    n_groups = db * (n_pages // n_step)

    def group(bi, hp, i):
        return jnp.minimum((bi * n_hq + hp) * nb + i, n_groups - 1)

    def page_spec(r):
        def index(bi, hp, i, pt):
            return (layer, pt[group(bi, hp, i) * n_step + r], 0, 0, 0)
        return pl.BlockSpec((1, 1, n_heads, HEAD_DIM, page_size), index)

    seq_of = lambda bi, hp, i: lax.div(group(bi, hp, i), n_pages // n_step)
    decode = dict(n_step=n_step, ppb=ppb, n_pages=n_pages, n_groups=n_groups, n_grid_steps=n_steps)
    return pl.pallas_call(
        functools.partial(_attn_kernel, decode=decode),
        grid_spec=pltpu.PrefetchScalarGridSpec(
            num_scalar_prefetch=1,
            grid=(b, n_hq, nb),
            in_specs=[
                pl.BlockSpec((1, hg, LANES, tq), lambda bi, hp, i, pt: (bi, hp, 0, i)),
                pl.BlockSpec((1, hg, nb, tq, LANES), lambda bi, hp, i, pt: (bi, hp, 0, 0, 0),
                             pipeline_mode=pl.Buffered(1)),
                pl.BlockSpec((1, hg, nb, LANES, tq), lambda bi, hp, i, pt: (bi, hp, 0, 0, 0),
                             pipeline_mode=pl.Buffered(1)),
                pl.BlockSpec((1, hg, LANES, LANES), lambda bi, hp, i, pt: (bi, hp, 0, 0),
                             pipeline_mode=pl.Buffered(1)),
            ] + [page_spec(r) for r in range(n_step)] + [
                pl.BlockSpec((1, n_heads, HEAD_DIM, 1),
                             lambda bi, hp, i, pt: (seq_of(bi, hp, i), 0, 0, 0)),
            ],
            out_specs=[
                pl.BlockSpec((1, tq, hg * HEAD_DIM), lambda bi, hp, i, pt: (bi, i, hp)),
                pl.BlockSpec((1, n_heads, LANES), lambda bi, hp, i, pt: (seq_of(bi, hp, i), 0, 0)),
            ],
            scratch_shapes=[
                pltpu.VMEM((hg, LANES, tq), BF16),
                pltpu.VMEM((hg, 1, tq), F32),
                pltpu.VMEM((hg, LANES, tq), F32),
                pltpu.VMEM((2, hg, ATTN_BLOCKS_PER_TRIP * MOBA_BLOCK, tq), BF16),
                pltpu.VMEM((n_heads, LANES), F32),
            ],
        ),
        out_shape=[jax.ShapeDtypeStruct((b, s, n_heads * HEAD_DIM), BF16),
                   jax.ShapeDtypeStruct((db, n_heads, LANES), F32)],
        compiler_params=pltpu.CompilerParams(
            dimension_semantics=("arbitrary", "arbitrary", "arbitrary"),
            vmem_limit_bytes=ATTN_VMEM_LIMIT),
        name="attn_prompt",
    )(page_table.reshape(-1), q, ka, vt, kmt, *([cache_kt] * n_step), dq4)


def _gate_table(kmean, n_heads):
    b, nb = kmean.shape[:2]
    t = kmean.reshape(b, nb, n_heads, HEAD_DIM).transpose(0, 2, 1, 3)
    return jnp.pad(t, ((0, 0), (0, 0), (HEAD_DIM, LANES - HEAD_DIM - nb), (0, LANES - HEAD_DIM)))


def _sproj_kernel(x_ref, nm_ref, win_ref, cos_ref, sin_ref, wpool_ref, pscale_ref, convw_ref,
                  spool_ref, sconv_ref, q_ref, k_ref, v_ref, pc_ref, u_ref, z_ref,
                  *, wa, wb, wc, pos):
    h = _rms(x_ref[...], nm_ref[...]).astype(BF16)
    cos, sin = cos_ref[...], sin_ref[...]
    scale = HEAD_DIM ** -0.5
    for c in range(wa // LANES):
        sl = slice(c * LANES, (c + 1) * LANES)
        q_ref[:, sl] = _rope_chunk(_dot(h, win_ref[:, sl]), cos, sin) * scale
        k_ref[:, sl] = _rope_chunk(_dot(h, win_ref[:, wa + c * LANES: wa + (c + 1) * LANES]), cos, sin)
        v_ref[:, sl] = _dot(h, win_ref[:, 2 * wa + c * LANES: 2 * wa + (c + 1) * LANES])
    o = 3 * wa
    u = _dot(h, win_ref[:, o:o + wb])
    hc = _dot(h, win_ref[:, o + wb:o + wb + wc])
    bg = _dot(h, win_ref[:, o + wb + wc:o + wb + 2 * wc])
    cg = _dot(h, win_ref[:, o + wb + 2 * wc:o + wb + 3 * wc])

    sums, run = {}, u
    for r in range(1, max(POOL_WINDOWS)):
        run = run + spool_ref[:, POOL_STATE - r, :]
        if r + 1 in POOL_WINDOWS:
            sums[r + 1] = run
    posv = jnp.full((u.shape[0], 1), pos, jnp.int32)
    d = _pool_select(sums[2], sums[4], sums[8], sums[16], posv) - u
    p = _dot(d.astype(BF16), wpool_ref[...]) * pscale_ref[...]

    z = cg * hc
    conv = (z * convw_ref[2:3, :] + sconv_ref[:, 1, :] * convw_ref[1:2, :]
            + sconv_ref[:, 0, :] * convw_ref[0:1, :])
    u_ref[...] = u
    z_ref[...] = z
    pc_ref[:, :wb] = p.astype(BF16)
    pc_ref[:, wb:] = (bg * conv).astype(BF16)


def _proj_sample(x, nm, w_in, cos, sin, wpool_bd, pscale, conv_w, spool, sconv, *, n_heads, pos):
    m, _ = x.shape
    wa = n_heads * HEAD_DIM
    wb = wpool_bd.shape[0]
    wc = conv_w.shape[1]
    assert spool.shape[1] == POOL_STATE and sconv.shape[1] == CONV_K - 1
    f = lambda w, dt: jax.ShapeDtypeStruct((m, w), dt)
    return pl.pallas_call(
        functools.partial(_sproj_kernel, wa=wa, wb=wb, wc=wc, pos=pos),
        out_shape=(f(wa, F32), f(wa, F32), f(wa, F32), f(wb + wc, BF16), f(wb, F32), f(wc, F32)),
        compiler_params=pltpu.CompilerParams(vmem_limit_bytes=VMEM_LIMIT),
        name="proj_sample",
    )(x, nm, w_in, cos, sin, wpool_bd, pscale, conv_w, spool, sconv)


def _stopk_kernel(gate_ref, idx_ref, *, n_blocks):
    gate = gate_ref[...]
    lane = lax.broadcasted_iota(jnp.int32, gate.shape, 1)
    lane_f = lane.astype(F32)
    gt = jnp.where(lane < n_blocks, gate, -jnp.inf)
    picks = jnp.zeros(gate.shape, F32)
    for r in range(MOBA_TOPK):
        mx = jnp.max(gt, axis=1, keepdims=True)
        first = jnp.min(jnp.where(gt == mx, lane_f, float(LANES)), axis=1, keepdims=True)
        gt = jnp.where(lane_f == first, -jnp.inf, gt)
        picks = jnp.where(lane == r, first, picks)
    idx_ref[...] = picks.astype(jnp.int32)


def _sample_topk(gates, n_blocks):
    assert MOBA_TOPK <= n_blocks <= LANES
    flat = gates.reshape(-1, LANES)
    return pl.pallas_call(
        functools.partial(_stopk_kernel, n_blocks=n_blocks),
        out_shape=jax.ShapeDtypeStruct(flat.shape, jnp.int32),
        name="sample_topk",
    )(flat).reshape(gates.shape)


def _sattn_kernel(idx_ref, pt_ref, *refs, n_fetch):
    k_refs = refs[:2 * n_fetch]
    v_refs = refs[2 * n_fetch:4 * n_fetch]
    q_ref, kn_ref, vn_ref, o_ref = refs[4 * n_fetch:]
    for hh in range(2):
        q = q_ref[0, hh]
        s_self = jnp.sum(q * kn_ref[0, hh], axis=0, keepdims=True)
        scores = [jnp.sum(k_refs[hh * n_fetch + t][0, 0, 0] * q, axis=0, keepdims=True)
                  for t in range(n_fetch)]
        m = s_self
        for s in scores:
            m = jnp.maximum(m, jnp.max(s, axis=1, keepdims=True))
        p_self = jnp.exp(s_self - m)
        l = p_self
        acc = p_self * vn_ref[0, hh]
        for t, s in enumerate(scores):
            p = jnp.exp(s - m)
            l = l + jnp.sum(p, axis=1, keepdims=True)
            acc = acc + jnp.sum(v_refs[hh * n_fetch + t][0, 0, 0] * p, axis=1, keepdims=True)
        o_ref[0, hh] = acc / l


def _sample_attn(cache_kt, cache_vt, page_table, idx, q4, kn4, vn4, *, layer):
    _, n_pool, n_heads, _, page_size = cache_kt.shape
    b = page_table.shape[0]
    ppb = MOBA_BLOCK // page_size
    n_fetch = MOBA_TOPK * ppb

    def page_spec(hh, r, t):
        def index(bi, hp, idx_s, pt):
            head = 2 * hp + hh
            blk = idx_s[(bi * n_heads + head) * MOBA_TOPK + r]
            return (layer, pt[bi, blk * ppb + t], head, 0, 0)
        return pl.BlockSpec((1, 1, 1, HEAD_DIM, page_size), index)

    fetch = [page_spec(hh, r, t) for hh in range(2) for r in range(MOBA_TOPK) for t in range(ppb)]
    vec = pl.BlockSpec((1, 2, HEAD_DIM, 1), lambda bi, hp, idx_s, pt: (bi, hp, 0, 0))
    return pl.pallas_call(
        functools.partial(_sattn_kernel, n_fetch=n_fetch),
        grid_spec=pltpu.PrefetchScalarGridSpec(
            num_scalar_prefetch=2,
            grid=(b, n_heads // 2),
            in_specs=fetch + fetch + [vec, vec, vec],
            out_specs=vec,
        ),
        out_shape=jax.ShapeDtypeStruct((b, n_heads, HEAD_DIM, 1), F32),
        compiler_params=_params(("parallel", "arbitrary")),
        name="sample_attn",
    )(idx.reshape(-1), page_table, *([cache_kt] * len(fetch)), *([cache_vt] * len(fetch)),
      q4, kn4, vn4)


def kernel(x_prompt, x_sample, cache_k, cache_v, page_table, state_pool, state_conv, norm_ffn1, ffn1_gate, ffn1_up, ffn1_down, norm_mix, w_in, w_pool, pool_scale, conv_w, w_o, norm_ffn2, ffn2_gate, ffn2_up, ffn2_down, norm_final):
    b, s, d = x_prompt.shape
    db, t_new, _ = x_sample.shape
    depth, n_pool, page_size, n_heads, head_dim = cache_k.shape
    assert head_dim == HEAD_DIM and t_new == 1 and n_heads % 2 == 0
    past_len = page_table.shape[1] * page_size
    assert past_len % MOBA_BLOCK == 0 and past_len + 1 >= max(POOL_WINDOWS)
    wa = n_heads * HEAD_DIM
    tm_prompt = 512 if (b * s) % 512 == 0 else MOBA_BLOCK

    cos_p, sin_p = _rope_tables(jnp.arange(s, dtype=jnp.int32))
    cos_s, sin_s = _rope_tables(past_len + jnp.arange(1, dtype=jnp.int32))
    row = lambda a: a.reshape(1, -1)
    bf = lambda a: a.astype(BF16)

    cache_kt = cache_k.transpose(0, 1, 3, 4, 2)
    cache_vt = cache_v.transpose(0, 1, 3, 4, 2)

    xp = x_prompt.reshape(b * s, d)
    xs = x_sample.reshape(db, d)
    outs = {n: [] for n in ("ks", "vs", "pp", "ps", "cp", "cs")}
    kv_all = None
    for l in range(depth):
        n_groups = w_pool.shape[1]
        wpool_bd = bf(jax.scipy.linalg.block_diag(*[w_pool[l, g] for g in range(n_groups)]))
        f1 = (row(norm_ffn1[l]), bf(ffn1_gate[l]), bf(ffn1_up[l]), bf(ffn1_down[l]))
        f2 = (row(norm_ffn2[l]), bf(ffn2_gate[l]), bf(ffn2_up[l]), bf(ffn2_down[l]))
        mixer_w = (row(norm_mix[l]), bf(w_in[l]))
        mixer_tail = (wpool_bd, row(pool_scale[l]), conv_w[l])
        wo = bf(w_o[l])

        last = l == depth - 1
        final = row(norm_final) if last else None

        xp = _ffn(xp, *f1, tm=tm_prompt)
        kp, vp, q, kt, va, pc, kmean, ptail, ctail = _proj_prompt(
            xp.reshape(b, s, d), *mixer_w, cos_p, sin_p, *mixer_tail, n_heads=n_heads,
            layer=l, depth=depth, kv_all=kv_all)
        kv_all = (kp, vp)
        xs = _ffn(xs, *f1, tm=db)
        qs, ksn, vsn, pcs, us, zs = _proj_sample(
            xs, *mixer_w, cos_s, sin_s, *mixer_tail, state_pool[l], state_conv[l],
            n_heads=n_heads, pos=past_len)
        col = lambda a: a.reshape(db, n_heads, HEAD_DIM, 1)

        a, gates = _attn_prompt(q, kt, va, _gate_table(kmean, n_heads), cache_kt, page_table,
                                col(qs), layer=l)
        xp = _ffn(xp, *f2, mix=(a.reshape(b * s, wa), pc.reshape(b * s, -1), wo), final=final,
                  tm=tm_prompt)
        outs["pp"].append(ptail[:, POOL_CARRY - POOL_STATE:])
        outs["cp"].append(ctail[:, CONV_CARRY - (CONV_K - 1):])

        idx = _sample_topk(gates, past_len // MOBA_BLOCK)
        a_s = _sample_attn(cache_kt, cache_vt, page_table, idx[:, :, :MOBA_TOPK],
                           col(qs), col(ksn), col(vsn), layer=l)
        xs = _ffn(xs, *f2, mix=(bf(a_s.reshape(db, wa)), pcs, wo), final=final, tm=db)
        outs["ks"].append(ksn.reshape(db, 1, n_heads, HEAD_DIM))
        outs["vs"].append(vsn.reshape(db, 1, n_heads, HEAD_DIM))
        outs["ps"].append(jnp.concatenate([state_pool[l][:, 1:], us[:, None]], axis=1))
        outs["cs"].append(jnp.concatenate([state_conv[l][:, 1:], zs[:, None]], axis=1))

    st = lambda n: jnp.stack(outs[n])
    k_prompt, v_prompt = (a.transpose(0, 1, 4, 2, 3) for a in kv_all)
    return (xp.reshape(b, s, d), xs.reshape(db, 1, d), k_prompt, v_prompt, st("ks"), st("vs"),
            st("pp"), st("ps"), st("cp"), st("cs"))
```

```python
import functools

import jax
import jax.numpy as jnp
from jax import lax
from jax.experimental import pallas as pl
from jax.experimental.pallas import tpu as pltpu

F32 = jnp.float32
BF16 = jnp.bfloat16

HEAD_DIM = 64
LANES = 128
MXU_DIM = 256
MOBA_BLOCK = 256
MOBA_TOPK = 3
POOL_WINDOWS = (2, 4, 8, 16)
POOL_STATE = max(POOL_WINDOWS) - 1
POOL_CARRY = 16
CONV_K = 3
CONV_CARRY = 8
ROPE_THETA = 10000.0
RMS_EPS = 1e-6
NEG = -1e30
VMEM_LIMIT = 56 * 1024 * 1024
ATTN_VMEM_LIMIT = 62 * 1024 * 1024


def _dot(a, b):
    return jnp.dot(a, b, preferred_element_type=F32)


def _rms(x, g):
    return x * lax.rsqrt(jnp.mean(x * x, axis=-1, keepdims=True) + RMS_EPS) * g


def _resident(shape):
    return pl.BlockSpec(shape, lambda *_: (0,) * len(shape), pipeline_mode=pl.Buffered(1))


def _layer_resident(stacked, layer):
    tail = stacked.shape[1:]
    return pl.BlockSpec((None,) + tail, lambda *_: (layer,) + (0,) * len(tail),
                        pipeline_mode=pl.Buffered(1))


def _params(sem):
    return pltpu.CompilerParams(dimension_semantics=sem, vmem_limit_bytes=VMEM_LIMIT)


def _ffn_kernel(*refs, premix, final_norm, ff_chunks):
    it = iter(refs)
    x_ref = next(it)
    if premix:
        a_ref, pc_ref, wo_ref = next(it), next(it), next(it)
    n_ref, wg_ref, wu_ref, wd_ref = next(it), next(it), next(it), next(it)
    if final_norm:
        nf_ref = next(it)
    o_ref = next(it)

    x = x_ref[...]
    if premix:
        wa = a_ref.shape[-1]
        x = x + _dot(a_ref[...], wo_ref[:wa, :]) + _dot(pc_ref[...], wo_ref[wa:, :])
    h = _rms(x, n_ref[...]).astype(BF16)
    y = jnp.zeros_like(x)
    for lo, hi in ff_chunks:
        g = _dot(h, wg_ref[:, lo:hi])
        u = _dot(h, wu_ref[:, lo:hi])
        act = (g * jax.nn.sigmoid(g) * u).astype(BF16)
        y = y + _dot(act, wd_ref[lo:hi, :])
    x = x + 0.5 * y
    if final_norm:
        x = _rms(x, nf_ref[...])
    o_ref[...] = x


def _ffn(x, norm, wg, wu, wd, *, layer, mix=None, final=None, tm):
    m, d = x.shape
    ff = wg.shape[2]
    res = lambda w: _layer_resident(w, layer)
    half = -(-(ff // 2) // MXU_DIM) * MXU_DIM
    ff_chunks = ((0, half), (half, ff))
    row = lambda w: pl.BlockSpec((tm, w), lambda i: (i, 0))
    args, specs = [x], [row(d)]
    if mix is not None:
        a, pc, wo = mix
        args += [a, pc, wo]
        specs += [row(a.shape[1]), row(pc.shape[1]), res(wo)]
    args += [norm, wg, wu, wd]
    specs += [res(norm), res(wg), res(wu), res(wd)]
    if final is not None:
        args.append(final)
        specs.append(_resident(final.shape))
    return pl.pallas_call(
        functools.partial(_ffn_kernel, premix=mix is not None, final_norm=final is not None,
                          ff_chunks=ff_chunks),
        grid=(m // tm,),
        in_specs=specs,
        out_specs=row(d),
        out_shape=jax.ShapeDtypeStruct((m, d), F32),
        compiler_params=_params(("parallel",)),
        name="ffn",
    )(*args)


def _rope_chunk(x, cos, sin_signed):
    lane = lax.broadcasted_iota(jnp.int32, x.shape, 1)
    first_half = (lane % HEAD_DIM) < (HEAD_DIM // 2)
    partner = jnp.where(first_half, pltpu.roll(x, LANES - HEAD_DIM // 2, 1),
                        pltpu.roll(x, HEAD_DIM // 2, 1))
    return x * cos + partner * sin_signed


def _pool_select(s2, s4, s8, s16, pos):
    lane = lax.broadcasted_iota(jnp.int32, s2.shape, 1)
    group = s2.shape[1] // len(POOL_WINDOWS)
    posf = (pos + 1).astype(F32)
    mean = lambda s, w: s / jnp.minimum(posf, float(w))
    return jnp.where(lane < group, mean(s2, 2),
                     jnp.where(lane < 2 * group, mean(s4, 4),
                               jnp.where(lane < 3 * group, mean(s8, 8), mean(s16, 16))))


def _rope_tables(pos):
    half = HEAD_DIM // 2
    inv = jnp.power(ROPE_THETA, -jnp.arange(half, dtype=F32) / half)
    ang = pos.astype(F32)[:, None] * inv[None, :]
    cos, sin = jnp.cos(ang), jnp.sin(ang)
    reps = LANES // HEAD_DIM
    return (jnp.tile(jnp.concatenate([cos, cos], axis=1), (1, reps)),
            jnp.tile(jnp.concatenate([-sin, sin], axis=1), (1, reps)))


def _proj_kernel(x_ref, nm_ref, win_ref, cos_ref, sin_ref, wpool_ref, pscale_ref, convw_ref,
                 *refs, wa, wb, wc):
    (k_ref, v_ref, qt_ref, ka_ref, vt_ref, pc_ref, km_ref, ptail_ref, ctail_ref,
     prevu_ref, prevz_ref) = refs[-11:]
    i = pl.program_id(1)
    tm = x_ref.shape[1]
    n_chunks = wa // LANES

    @pl.when(i == 0)
    def _():
        prevu_ref[...] = jnp.zeros_like(prevu_ref)
        prevz_ref[...] = jnp.zeros_like(prevz_ref)

    h = _rms(x_ref[0], nm_ref[...]).astype(BF16)
    cos, sin = cos_ref[...], sin_ref[...]
    lane = lax.broadcasted_iota(jnp.int32, (tm, LANES), 1)
    low = lane < HEAD_DIM
    block_onehot = (lane - HEAD_DIM == i).astype(F32)
    row_t = lax.broadcasted_iota(jnp.int32, (HEAD_DIM, tm), 0)
    zeros_t = jnp.zeros((HEAD_DIM, tm), BF16)
    ones_row_t = (row_t == 0).astype(BF16)
    scale = HEAD_DIM ** -0.5

    wide = {}
    for c in range(n_chunks):
        sl = slice(c * LANES, (c + 1) * LANES)
        if c % (MXU_DIM // LANES) == 0:
            cols = slice(c * LANES, c * LANES + MXU_DIM)
            wide = {name: _dot(h, win_ref[:, off + cols.start:off + cols.stop])
                    for name, off in (("q", 0), ("k", wa), ("v", 2 * wa))}
        part = slice((c * LANES) % MXU_DIM, (c * LANES) % MXU_DIM + LANES)
        q = _rope_chunk(wide["q"][:, part], cos, sin) * scale
        k = _rope_chunk(wide["k"][:, part], cos, sin)
        v = wide["v"][:, part]
        km_ref[0, 0, :, sl] = jnp.mean(k, axis=0, keepdims=True)
        q_t, k_t, v_t = q.T, k.T, v.T
        for hh in range(2):
            head = 2 * c + hh
            rows = slice(hh * HEAD_DIM, (hh + 1) * HEAD_DIM)
            k_ref[0, 0, head] = k_t[rows]
            v_ref[0, 0, head] = v_t[rows]
            qt_ref[0, head] = jnp.concatenate([q_t[rows].astype(BF16), zeros_t], axis=0)
            vt_ref[0, head, 0] = jnp.concatenate([v_t[rows].astype(BF16), ones_row_t], axis=0)
            kh = k if hh == 0 else pltpu.roll(k, HEAD_DIM, 1)
            ka_ref[0, head, 0] = jnp.where(low, kh, block_onehot).astype(BF16)

    o = 3 * wa
    u = _dot(h, win_ref[:, o:o + wb])
    hc = _dot(h, win_ref[:, o + wb:o + wb + wc])
    bg = _dot(h, win_ref[:, o + wb + wc:o + wb + 2 * wc])
    cg = _dot(h, win_ref[:, o + wb + 2 * wc:o + wb + 3 * wc])

    pos = i * tm + lax.broadcasted_iota(jnp.int32, (tm, 1), 0)
    e = jnp.concatenate([prevu_ref[...], u], axis=0)
    s2 = e + pltpu.roll(e, 1, 0)
    s4 = s2 + pltpu.roll(s2, 2, 0)
    s8 = s4 + pltpu.roll(s4, 4, 0)
    s16 = s8 + pltpu.roll(s8, 8, 0)
    t = slice(POOL_CARRY, POOL_CARRY + tm)
    d = _pool_select(s2[t], s4[t], s8[t], s16[t], pos) - u
    p = _dot(d.astype(BF16), wpool_ref[...]) * pscale_ref[...]
    prevu_ref[...] = u[tm - POOL_CARRY:]
    ptail_ref[0] = u[tm - POOL_CARRY:]

    z = cg * hc
    ze = jnp.concatenate([prevz_ref[...], z], axis=0)
    tz = slice(CONV_CARRY, CONV_CARRY + tm)
    conv = (z * convw_ref[2:3, :] + pltpu.roll(ze, 1, 0)[tz] * convw_ref[1:2, :]
            + pltpu.roll(ze, 2, 0)[tz] * convw_ref[0:1, :])
    prevz_ref[...] = z[tm - CONV_CARRY:]
    ctail_ref[0] = z[tm - CONV_CARRY:]
    pc_ref[:, :wb] = p.astype(BF16)
    pc_ref[:, wb:] = (bg * conv).astype(BF16)


def _proj_prompt(x, nm, w_in, cos, sin, wpool_bd, pscale, conv_w, *, n_heads, layer, depth,
                 kv_all=None):
    b, s, d = x.shape
    tm = MOBA_BLOCK
    nb = s // tm
    wa = n_heads * HEAD_DIM
    wb = wpool_bd.shape[1]
    wc = conv_w.shape[2]
    assert s % tm == 0 and nb <= LANES - HEAD_DIM and wa % MXU_DIM == 0
    assert w_in.shape[2] == 3 * wa + wb + 3 * wc
    seq = lambda w: pl.BlockSpec((1, tm, w), lambda bi, i: (bi, i, 0))
    tab = pl.BlockSpec((tm, LANES), lambda bi, i: (i, 0))
    per_b = lambda r, w: pl.BlockSpec((1, r, w), lambda bi, i: (bi, 0, 0))
    head_t = pl.BlockSpec((1, 1, n_heads, HEAD_DIM, tm), lambda bi, i: (layer, bi, 0, 0, i))
    out_shape = (
        jax.ShapeDtypeStruct((depth, b, n_heads, HEAD_DIM, s), F32),
        jax.ShapeDtypeStruct((depth, b, n_heads, HEAD_DIM, s), F32),
        jax.ShapeDtypeStruct((b, n_heads, LANES, s), BF16),
        jax.ShapeDtypeStruct((b, n_heads, nb, tm, LANES), BF16),
        jax.ShapeDtypeStruct((b, n_heads, nb, LANES, tm), BF16),
        jax.ShapeDtypeStruct((b * s, wb + wc), BF16),
        jax.ShapeDtypeStruct((b, nb, 1, wa), F32),
        jax.ShapeDtypeStruct((b, POOL_CARRY, wb), F32),
        jax.ShapeDtypeStruct((b, CONV_CARRY, wc), F32),
    )
    out_specs = (
        head_t, head_t,
        pl.BlockSpec((1, n_heads, LANES, tm), lambda bi, i: (bi, 0, 0, i)),
        pl.BlockSpec((1, n_heads, 1, tm, LANES), lambda bi, i: (bi, 0, i, 0, 0)),
        pl.BlockSpec((1, n_heads, 1, LANES, tm), lambda bi, i: (bi, 0, i, 0, 0)),
        pl.BlockSpec((tm, wb + wc), lambda bi, i: (bi * nb + i, 0)),
        pl.BlockSpec((1, 1, 1, wa), lambda bi, i: (bi, i, 0, 0)),
        per_b(POOL_CARRY, wb), per_b(CONV_CARRY, wc),
    )
    args = [x, nm, w_in, cos, sin, wpool_bd, pscale, conv_w]
    res = lambda w: _layer_resident(w, layer)
    in_specs = [seq(d), res(nm), res(w_in), tab, tab, res(wpool_bd), res(pscale), res(conv_w)]
    aliases = {}
    if kv_all is not None:
        aliases = {len(args): 0, len(args) + 1: 1}
        args += list(kv_all)
        in_specs += [pl.BlockSpec(memory_space=pl.ANY)] * 2
    return pl.pallas_call(
        functools.partial(_proj_kernel, wa=wa, wb=wb, wc=wc),
        grid=(b, nb),
        in_specs=in_specs,
        out_specs=out_specs,
        out_shape=out_shape,
        scratch_shapes=[pltpu.VMEM((POOL_CARRY, wb), F32), pltpu.VMEM((CONV_CARRY, wc), F32)],
        input_output_aliases=aliases,
        compiler_params=_params(("parallel", "arbitrary")),
        name="proj_prompt",
    )(*args)


def _top_blocks_t(gate_t, n_valid):
    blk = lax.broadcasted_iota(jnp.int32, gate_t.shape, 0)
    blk_f = blk.astype(F32)
    g = jnp.where(blk < n_valid, gate_t, -jnp.inf)
    picked = jnp.zeros(gate_t.shape, jnp.bool_)
    for _ in range(MOBA_TOPK):
        m = jnp.max(g, axis=0, keepdims=True)
        first = jnp.min(jnp.where(g == m, blk_f, float(2 * LANES)), axis=0, keepdims=True)
        hit = (blk_f == first) & (m > -jnp.inf)
        picked = picked | hit
        g = jnp.where(hit, -jnp.inf, g)
    return jnp.where(picked, 0.0, NEG)


def _logits(a, b):
    return jnp.dot(a, b, preferred_element_type=F32).astype(BF16)


ATTN_BLOCKS_PER_TRIP = 2
ATTN_HEADS_PER_STEP = 8


def _decode_gates_step(page_refs, dq_ref, gate_ref, gacc_ref, lin, *, pages_per_block, n_pages,
                       n_groups, n_grid_steps):
    n_step = len(page_refs)
    n_heads = dq_ref.shape[1]

    @pl.when(lin == 0)
    def _():
        gacc_ref[...] = jnp.zeros_like(gacc_ref)

    def accumulate():
        first_page = lax.rem(lin * n_step, n_pages)
        first_block = lax.div(first_page, pages_per_block)
        lane = lax.broadcasted_iota(jnp.int32, (n_heads, LANES), 1)
        q = dq_ref[0]
        gate = jnp.where(first_page == 0, 0.0, gacc_ref[...])
        for r in range(n_step // pages_per_block):
            tot = page_refs[r * pages_per_block][0, 0]
            for t in range(1, pages_per_block):
                tot = tot + page_refs[r * pages_per_block + t][0, 0]
            k_sum = jnp.sum(tot, axis=2, keepdims=True)
            val = jnp.sum(k_sum * q, axis=1) / float(MOBA_BLOCK)
            gate = jnp.where(lane == first_block + r, val, gate)
        gacc_ref[...] = gate
        gate_ref[0] = gate

    if n_groups == n_grid_steps:
        accumulate()
    else:
        pl.when(lin < n_groups)(accumulate)


def _attn_kernel(pt_ref, q_ref, ka_ref, vt_ref, kmr_ref, *refs, decode):
    n_step = decode["n_step"]
    page_refs, dq_ref = refs[:n_step], refs[n_step]
    o_ref, gate_ref, qs_ref, m_ref, acc_ref, s_ref, gacc_ref = refs[n_step + 1:]
    lin = (pl.program_id(0) * pl.num_programs(1) + pl.program_id(1)) * pl.num_programs(2) \
        + pl.program_id(2)
    _decode_gates_step(page_refs, dq_ref, gate_ref, gacc_ref, lin, pages_per_block=decode["ppb"],
                       n_pages=decode["n_pages"], n_groups=decode["n_groups"],
                       n_grid_steps=decode["n_grid_steps"])

    i = pl.program_id(2)
    tq = q_ref.shape[3]
    nb = ka_ref.shape[2]
    nbp = -(-nb // 8) * 8
    key = lax.broadcasted_iota(jnp.int32, (MOBA_BLOCK, tq), 0)
    qry = lax.broadcasted_iota(jnp.int32, (MOBA_BLOCK, tq), 1)
    n_h = q_ref.shape[1]
    g = ATTN_BLOCKS_PER_TRIP
    last_group = nb // g - 1

    def store_logits(slot, jj, hh):
        keys = ka_ref[0, hh, pl.ds(g * jj, g)].reshape(g * MOBA_BLOCK, LANES)
        s_ref[slot, hh] = _logits(keys, qs_ref[hh])

    heads = range(n_h)
    gates, diag, probs = [], [], []
    for hh in heads:
        km = kmr_ref[0, hh, HEAD_DIM:HEAD_DIM + nbp]
        km_hi = km.astype(BF16)
        km_lo = (km - km_hi.astype(F32)).astype(BF16)
        gates.append(_dot(km_hi, q_ref[0, hh]) + _dot(km_lo, q_ref[0, hh]))
    for hh in heads:
        diag.append(jnp.where(key <= qry, _logits(ka_ref[0, hh, i], q_ref[0, hh]), NEG))
    for hh in heads:
        pen_t = _top_blocks_t(gates[hh], i)
        pen_t = jnp.concatenate([jnp.zeros((HEAD_DIM, tq), F32), pen_t,
                                 jnp.zeros((LANES - HEAD_DIM - nbp, tq), F32)], axis=0)
        qs_ref[hh] = (q_ref[0, hh].astype(F32) + pen_t).astype(BF16)
    for hh in heads:
        m = jnp.max(diag[hh], axis=0, keepdims=True)
        m_ref[hh] = m.astype(F32)
        probs.append(jnp.exp(diag[hh] - m))
    for hh in heads:
        acc_ref[hh] = _dot(vt_ref[0, hh, i], probs[hh])
    for hh in heads:
        store_logits(0, 0, hh)

    n_trips = (i + g - 1) // g

    def trip(jj, slot, prefetch):
        nxt = jnp.minimum(jj + 1, last_group)
        for hh in range(n_h):
            s = s_ref[slot, hh]
            m = m_ref[hh]
            m_new = jnp.maximum(m, jnp.max(s, axis=0, keepdims=True).astype(F32))
            p = jnp.exp(s - m_new.astype(BF16))
            v = jnp.concatenate([vt_ref[0, hh, g * jj + t] for t in range(g)], axis=1)
            acc_ref[hh] = acc_ref[hh] * jnp.exp(m - m_new) + _dot(v, p)
            m_ref[hh] = m_new
            if prefetch:
                store_logits(1 - slot, nxt, hh)

    @pl.loop(0, n_trips // 2)
    def _(tt):
        trip(2 * tt, 0, True)
        trip(2 * tt + 1, 1, True)

    @pl.when(n_trips % 2 == 1)
    def _():
        trip(n_trips - 1, 0, False)

    for p in range(n_h // 2):
        pair = [acc_ref[hh][:HEAD_DIM] / acc_ref[hh][HEAD_DIM:HEAD_DIM + 1] for hh in (2 * p, 2 * p + 1)]
        o_ref[:, p * LANES:(p + 1) * LANES] = jnp.concatenate(pair, axis=0).T.astype(BF16)


def _attn_prompt(q, ka, vt, kmt, cache_kt, page_table, dq4, *, layer):
    b, n_heads, _, s = q.shape
    nb = ka.shape[2]
    tq = MOBA_BLOCK
    hg = ATTN_HEADS_PER_STEP
    assert nb % ATTN_BLOCKS_PER_TRIP == 0 and n_heads % hg == 0 and hg % 2 == 0
    n_hq = n_heads // hg
    n_steps = b * n_hq * nb

    page_size = cache_kt.shape[-1]
    db, n_pages = page_table.shape
    ppb = MOBA_BLOCK // page_size
    assert MOBA_BLOCK % page_size == 0 and n_pages % ppb == 0 and n_pages // ppb <= LANES
    n_step = next(p for p in range(ppb, n_pages + 1, ppb)
                  if n_pages % p == 0 and db * (n_pages // p) <= n_steps)
    n_groups = db * (n_pages // n_step)

    def group(bi, hp, i):
        return jnp.minimum((bi * n_hq + hp) * nb + i, n_groups - 1)

    def page_spec(r):
        def index(bi, hp, i, pt):
            return (layer, pt[group(bi, hp, i) * n_step + r], 0, 0, 0)
        return pl.BlockSpec((1, 1, n_heads, HEAD_DIM, page_size), index)

    seq_of = lambda bi, hp, i: lax.div(group(bi, hp, i), n_pages // n_step)
    decode = dict(n_step=n_step, ppb=ppb, n_pages=n_pages, n_groups=n_groups, n_grid_steps=n_steps)
    return pl.pallas_call(
        functools.partial(_attn_kernel, decode=decode),
        grid_spec=pltpu.PrefetchScalarGridSpec(
            num_scalar_prefetch=1,
            grid=(b, n_hq, nb),
            in_specs=[
                pl.BlockSpec((1, hg, LANES, tq), lambda bi, hp, i, pt: (bi, hp, 0, i)),
                pl.BlockSpec((1, hg, nb, tq, LANES), lambda bi, hp, i, pt: (bi, hp, 0, 0, 0),
                             pipeline_mode=pl.Buffered(1)),
                pl.BlockSpec((1, hg, nb, LANES, tq), lambda bi, hp, i, pt: (bi, hp, 0, 0, 0),
                             pipeline_mode=pl.Buffered(1)),
                pl.BlockSpec((1, hg, LANES, LANES), lambda bi, hp, i, pt: (bi, hp, 0, 0),
                             pipeline_mode=pl.Buffered(1)),
            ] + [page_spec(r) for r in range(n_step)] + [
                pl.BlockSpec((1, n_heads, HEAD_DIM, 1),
                             lambda bi, hp, i, pt: (seq_of(bi, hp, i), 0, 0, 0)),
            ],
            out_specs=[
                pl.BlockSpec((tq, hg * HEAD_DIM), lambda bi, hp, i, pt: (bi * nb + i, hp)),
                pl.BlockSpec((1, n_heads, LANES), lambda bi, hp, i, pt: (seq_of(bi, hp, i), 0, 0)),
            ],
            scratch_shapes=[
                pltpu.VMEM((hg, LANES, tq), BF16),
                pltpu.VMEM((hg, 1, tq), F32),
                pltpu.VMEM((hg, LANES, tq), F32),
                pltpu.VMEM((2, hg, ATTN_BLOCKS_PER_TRIP * MOBA_BLOCK, tq), BF16),
                pltpu.VMEM((n_heads, LANES), F32),
            ],
        ),
        out_shape=[jax.ShapeDtypeStruct((b * s, n_heads * HEAD_DIM), BF16),
                   jax.ShapeDtypeStruct((db, n_heads, LANES), F32)],
        compiler_params=pltpu.CompilerParams(
            dimension_semantics=("arbitrary", "arbitrary", "arbitrary"),
            vmem_limit_bytes=ATTN_VMEM_LIMIT),
        name="attn_prompt",
    )(page_table.reshape(-1), q, ka, vt, kmt, *([cache_kt] * n_step), dq4)


def _gate_table(kmean, n_heads):
    b, nb = kmean.shape[:2]
    t = kmean.reshape(b, nb, n_heads, HEAD_DIM).transpose(0, 2, 1, 3)
    return jnp.pad(t, ((0, 0), (0, 0), (HEAD_DIM, LANES - HEAD_DIM - nb), (0, LANES - HEAD_DIM)))


def _sproj_kernel(x_ref, nm_ref, win_ref, cos_ref, sin_ref, wpool_ref, pscale_ref, convw_ref,
                  spool_ref, sconv_ref, q_ref, k_ref, v_ref, pc_ref, u_ref, z_ref,
                  *, wa, wb, wc, pos):
    h = _rms(x_ref[...], nm_ref[...]).astype(BF16)
    cos, sin = cos_ref[...], sin_ref[...]
    scale = HEAD_DIM ** -0.5
    for c in range(wa // LANES):
        sl = slice(c * LANES, (c + 1) * LANES)
        q_ref[:, sl] = _rope_chunk(_dot(h, win_ref[:, sl]), cos, sin) * scale
        k_ref[:, sl] = _rope_chunk(_dot(h, win_ref[:, wa + c * LANES: wa + (c + 1) * LANES]), cos, sin)
        v_ref[:, sl] = _dot(h, win_ref[:, 2 * wa + c * LANES: 2 * wa + (c + 1) * LANES])
    o = 3 * wa
    u = _dot(h, win_ref[:, o:o + wb])
    hc = _dot(h, win_ref[:, o + wb:o + wb + wc])
    bg = _dot(h, win_ref[:, o + wb + wc:o + wb + 2 * wc])
    cg = _dot(h, win_ref[:, o + wb + 2 * wc:o + wb + 3 * wc])

    sums, run = {}, u
    for r in range(1, max(POOL_WINDOWS)):
        run = run + spool_ref[:, POOL_STATE - r, :]
        if r + 1 in POOL_WINDOWS:
            sums[r + 1] = run
    posv = jnp.full((u.shape[0], 1), pos, jnp.int32)
    d = _pool_select(sums[2], sums[4], sums[8], sums[16], posv) - u
    p = _dot(d.astype(BF16), wpool_ref[...]) * pscale_ref[...]

    z = cg * hc
    conv = (z * convw_ref[2:3, :] + sconv_ref[:, 1, :] * convw_ref[1:2, :]
            + sconv_ref[:, 0, :] * convw_ref[0:1, :])
    u_ref[...] = u
    z_ref[...] = z
    pc_ref[:, :wb] = p.astype(BF16)
    pc_ref[:, wb:] = (bg * conv).astype(BF16)


def _proj_sample(x, nm, w_in, cos, sin, wpool_bd, pscale, conv_w, spool, sconv, *, n_heads, pos):
    m, _ = x.shape
    wa = n_heads * HEAD_DIM
    wb = wpool_bd.shape[0]
    wc = conv_w.shape[1]
    assert spool.shape[1] == POOL_STATE and sconv.shape[1] == CONV_K - 1
    f = lambda w, dt: jax.ShapeDtypeStruct((m, w), dt)
    return pl.pallas_call(
        functools.partial(_sproj_kernel, wa=wa, wb=wb, wc=wc, pos=pos),
        out_shape=(f(wa, F32), f(wa, F32), f(wa, F32), f(wb + wc, BF16), f(wb, F32), f(wc, F32)),
        compiler_params=pltpu.CompilerParams(vmem_limit_bytes=VMEM_LIMIT),
        name="proj_sample",
    )(x, nm, w_in, cos, sin, wpool_bd, pscale, conv_w, spool, sconv)


def _stopk_kernel(gate_ref, idx_ref, *, n_blocks):
    gate = gate_ref[...]
    lane = lax.broadcasted_iota(jnp.int32, gate.shape, 1)
    lane_f = lane.astype(F32)
    gt = jnp.where(lane < n_blocks, gate, -jnp.inf)
    picks = jnp.zeros(gate.shape, F32)
    for r in range(MOBA_TOPK):
        mx = jnp.max(gt, axis=1, keepdims=True)
        first = jnp.min(jnp.where(gt == mx, lane_f, float(LANES)), axis=1, keepdims=True)
        gt = jnp.where(lane_f == first, -jnp.inf, gt)
        picks = jnp.where(lane == r, first, picks)
    idx_ref[...] = picks.astype(jnp.int32)


def _sample_topk(gates, n_blocks):
    assert MOBA_TOPK <= n_blocks <= LANES
    flat = gates.reshape(-1, LANES)
    return pl.pallas_call(
        functools.partial(_stopk_kernel, n_blocks=n_blocks),
        out_shape=jax.ShapeDtypeStruct(flat.shape, jnp.int32),
        name="sample_topk",
    )(flat).reshape(gates.shape)


def _sattn_kernel(idx_ref, pt_ref, *refs, n_fetch):
    k_refs = refs[:2 * n_fetch]
    v_refs = refs[2 * n_fetch:4 * n_fetch]
    q_ref, kn_ref, vn_ref, o_ref = refs[4 * n_fetch:]
    for hh in range(2):
        q = q_ref[0, hh]
        s_self = jnp.sum(q * kn_ref[0, hh], axis=0, keepdims=True)
        scores = [jnp.sum(k_refs[hh * n_fetch + t][0, 0, 0] * q, axis=0, keepdims=True)
                  for t in range(n_fetch)]
        m = s_self
        for s in scores:
            m = jnp.maximum(m, jnp.max(s, axis=1, keepdims=True))
        p_self = jnp.exp(s_self - m)
        l = p_self
        acc = p_self * vn_ref[0, hh]
        for t, s in enumerate(scores):
            p = jnp.exp(s - m)
            l = l + jnp.sum(p, axis=1, keepdims=True)
            acc = acc + jnp.sum(v_refs[hh * n_fetch + t][0, 0, 0] * p, axis=1, keepdims=True)
        o_ref[0, hh] = acc / l


def _sample_attn(cache_kt, cache_vt, page_table, idx, q4, kn4, vn4, *, layer):
    _, n_pool, n_heads, _, page_size = cache_kt.shape
    b = page_table.shape[0]
    ppb = MOBA_BLOCK // page_size
    n_fetch = MOBA_TOPK * ppb

    def page_spec(hh, r, t):
        def index(bi, hp, idx_s, pt):
            head = 2 * hp + hh
            blk = idx_s[(bi * n_heads + head) * MOBA_TOPK + r]
            return (layer, pt[bi, blk * ppb + t], head, 0, 0)
        return pl.BlockSpec((1, 1, 1, HEAD_DIM, page_size), index)

    fetch = [page_spec(hh, r, t) for hh in range(2) for r in range(MOBA_TOPK) for t in range(ppb)]
    vec = pl.BlockSpec((1, 2, HEAD_DIM, 1), lambda bi, hp, idx_s, pt: (bi, hp, 0, 0))
    return pl.pallas_call(
        functools.partial(_sattn_kernel, n_fetch=n_fetch),
        grid_spec=pltpu.PrefetchScalarGridSpec(
            num_scalar_prefetch=2,
            grid=(b, n_heads // 2),
            in_specs=fetch + fetch + [vec, vec, vec],
            out_specs=vec,
        ),
        out_shape=jax.ShapeDtypeStruct((b, n_heads, HEAD_DIM, 1), F32),
        compiler_params=_params(("parallel", "arbitrary")),
        name="sample_attn",
    )(idx.reshape(-1), page_table, *([cache_kt] * len(fetch)), *([cache_vt] * len(fetch)),
      q4, kn4, vn4)


def kernel(x_prompt, x_sample, cache_k, cache_v, page_table, state_pool, state_conv, norm_ffn1, ffn1_gate, ffn1_up, ffn1_down, norm_mix, w_in, w_pool, pool_scale, conv_w, w_o, norm_ffn2, ffn2_gate, ffn2_up, ffn2_down, norm_final):
    b, s, d = x_prompt.shape
    db, t_new, _ = x_sample.shape
    depth, n_pool, page_size, n_heads, head_dim = cache_k.shape
    assert head_dim == HEAD_DIM and t_new == 1 and n_heads % 2 == 0
    past_len = page_table.shape[1] * page_size
    assert past_len % MOBA_BLOCK == 0 and past_len + 1 >= max(POOL_WINDOWS)
    wa = n_heads * HEAD_DIM
    tm_prompt = 512 if (b * s) % 512 == 0 else MOBA_BLOCK

    cos_p, sin_p = _rope_tables(jnp.arange(s, dtype=jnp.int32))
    cos_s, sin_s = _rope_tables(past_len + jnp.arange(1, dtype=jnp.int32))
    row = lambda a: a.reshape(1, -1)
    bf = lambda a: a.astype(BF16)

    cache_kt = cache_k.transpose(0, 1, 3, 4, 2)
    cache_vt = cache_v.transpose(0, 1, 3, 4, 2)

    xp = x_prompt.reshape(b * s, d)
    xs = x_sample.reshape(db, d)
    outs = {n: [] for n in ("ks", "vs", "pp", "ps", "cp", "cs")}
    kv_all = None
    rows = lambda a: a.reshape(depth, 1, -1)
    n_groups = w_pool.shape[1]
    wpool_bd = bf(jnp.stack([jax.scipy.linalg.block_diag(*[w_pool[l, g] for g in range(n_groups)])
                             for l in range(depth)]))
    f1 = (rows(norm_ffn1), bf(ffn1_gate), bf(ffn1_up), bf(ffn1_down))
    f2 = (rows(norm_ffn2), bf(ffn2_gate), bf(ffn2_up), bf(ffn2_down))
    mixer = (rows(norm_mix), bf(w_in), wpool_bd, rows(pool_scale), conv_w)
    wo = bf(w_o)
    for l in range(depth):
        last = l == depth - 1
        final = row(norm_final) if last else None
        nm_l, w_in_l, wpool_l, pscale_l, convw_l = (w[l] for w in mixer)

        xp = _ffn(xp, *f1, layer=l, tm=tm_prompt)
        kp, vp, q, ka, vt, pc, kmean, ptail, ctail = _proj_prompt(
            xp.reshape(b, s, d), mixer[0], mixer[1], cos_p, sin_p, *mixer[2:], n_heads=n_heads,
            layer=l, depth=depth, kv_all=kv_all)
        kv_all = (kp, vp)
        xs = _ffn(xs, *f1, layer=l, tm=db)
        qs, ksn, vsn, pcs, us, zs = _proj_sample(
            xs, nm_l, w_in_l, cos_s, sin_s, wpool_l, pscale_l, convw_l, state_pool[l],
            state_conv[l], n_heads=n_heads, pos=past_len)
        col = lambda a: a.reshape(db, n_heads, HEAD_DIM, 1)

        a, gates = _attn_prompt(q, ka, vt, _gate_table(kmean, n_heads), cache_kt, page_table,
                                col(qs), layer=l)
        xp = _ffn(xp, *f2, layer=l, mix=(a, pc, wo), final=final, tm=tm_prompt)
        outs["pp"].append(ptail[:, POOL_CARRY - POOL_STATE:])
        outs["cp"].append(ctail[:, CONV_CARRY - (CONV_K - 1):])

        idx = _sample_topk(gates, past_len // MOBA_BLOCK)
        a_s = _sample_attn(cache_kt, cache_vt, page_table, idx[:, :, :MOBA_TOPK],
                           col(qs), col(ksn), col(vsn), layer=l)
        xs = _ffn(xs, *f2, layer=l, mix=(bf(a_s.reshape(db, wa)), pcs, wo), final=final, tm=db)
        outs["ks"].append(ksn.reshape(db, 1, n_heads, HEAD_DIM))
        outs["vs"].append(vsn.reshape(db, 1, n_heads, HEAD_DIM))
        outs["ps"].append(jnp.concatenate([state_pool[l][:, 1:], us[:, None]], axis=1))
        outs["cs"].append(jnp.concatenate([state_conv[l][:, 1:], zs[:, None]], axis=1))

    st = lambda n: jnp.stack(outs[n])
    k_prompt, v_prompt = (a.transpose(0, 1, 4, 2, 3) for a in kv_all)
    return (xp.reshape(b, s, d), xs.reshape(db, 1, d), k_prompt, v_prompt, st("ks"), st("vs"),
            st("pp"), st("ps"), st("cp"), st("cs"))
```

```python
import functools

import jax
import jax.numpy as jnp
from jax import lax
from jax.experimental import pallas as pl
from jax.experimental.pallas import tpu as pltpu

F32 = jnp.float32
BF16 = jnp.bfloat16

HEAD_DIM = 64
LANES = 128
MXU_DIM = 256
MOBA_BLOCK = 256
MOBA_TOPK = 3
POOL_WINDOWS = (2, 4, 8, 16)
POOL_STATE = max(POOL_WINDOWS) - 1
POOL_CARRY = 16
CONV_K = 3
CONV_CARRY = 8
ROPE_THETA = 10000.0
RMS_EPS = 1e-6
NEG = -1e30
VMEM_LIMIT = 56 * 1024 * 1024
ATTN_VMEM_LIMIT = 62 * 1024 * 1024


def _dot(a, b):
    return jnp.dot(a, b, preferred_element_type=F32)


def _rms(x, g):
    return x * lax.rsqrt(jnp.mean(x * x, axis=-1, keepdims=True) + RMS_EPS) * g


def _resident(shape):
    return pl.BlockSpec(shape, lambda *_: (0,) * len(shape), pipeline_mode=pl.Buffered(1))


def _layer_resident(stacked, layer):
    tail = stacked.shape[1:]
    return pl.BlockSpec((None,) + tail, lambda *_: (layer,) + (0,) * len(tail),
                        pipeline_mode=pl.Buffered(1))


def _params(sem):
    return pltpu.CompilerParams(dimension_semantics=sem, vmem_limit_bytes=VMEM_LIMIT)


def _ffn_kernel(*refs, premix, final_norm, ff_chunks):
    it = iter(refs)
    x_ref = next(it)
    if premix:
        a_ref, pc_ref, wo_ref = next(it), next(it), next(it)
    n_ref, wg_ref, wu_ref, wd_ref = next(it), next(it), next(it), next(it)
    if final_norm:
        nf_ref = next(it)
    o_ref = next(it)

    x = x_ref[...]
    if premix:
        wa = a_ref.shape[-1]
        x = x + _dot(a_ref[...], wo_ref[:wa, :]) + _dot(pc_ref[...], wo_ref[wa:, :])
    h = _rms(x, n_ref[...]).astype(BF16)
    y = jnp.zeros_like(x)
    for lo, hi in ff_chunks:
        g = _dot(h, wg_ref[:, lo:hi])
        u = _dot(h, wu_ref[:, lo:hi])
        act = (g * jax.nn.sigmoid(g) * u).astype(BF16)
        y = y + _dot(act, wd_ref[lo:hi, :])
    x = x + 0.5 * y
    if final_norm:
        x = _rms(x, nf_ref[...])
    o_ref[...] = x


def _ffn(x, norm, wg, wu, wd, *, layer, mix=None, final=None, tm):
    m, d = x.shape
    ff = wg.shape[2]
    res = lambda w: _layer_resident(w, layer)
    half = -(-(ff // 2) // MXU_DIM) * MXU_DIM
    ff_chunks = ((0, half), (half, ff))
    row = lambda w: pl.BlockSpec((tm, w), lambda i: (i, 0))
    args, specs = [x], [row(d)]
    if mix is not None:
        a, pc, wo = mix
        args += [a, pc, wo]
        specs += [row(a.shape[1]), row(pc.shape[1]), res(wo)]
    args += [norm, wg, wu, wd]
    specs += [res(norm), res(wg), res(wu), res(wd)]
    if final is not None:
        args.append(final)
        specs.append(_resident(final.shape))
    return pl.pallas_call(
        functools.partial(_ffn_kernel, premix=mix is not None, final_norm=final is not None,
                          ff_chunks=ff_chunks),
        grid=(m // tm,),
        in_specs=specs,
        out_specs=row(d),
        out_shape=jax.ShapeDtypeStruct((m, d), F32),
        compiler_params=_params(("parallel",)),
        name="ffn",
    )(*args)


def _rope_chunk(x, cos, sin_signed):
    lane = lax.broadcasted_iota(jnp.int32, x.shape, 1)
    first_half = (lane % HEAD_DIM) < (HEAD_DIM // 2)
    partner = jnp.where(first_half, pltpu.roll(x, LANES - HEAD_DIM // 2, 1),
                        pltpu.roll(x, HEAD_DIM // 2, 1))
    return x * cos + partner * sin_signed


def _pool_select(s2, s4, s8, s16, pos):
    lane = lax.broadcasted_iota(jnp.int32, s2.shape, 1)
    group = s2.shape[1] // len(POOL_WINDOWS)
    posf = (pos + 1).astype(F32)
    mean = lambda s, w: s / jnp.minimum(posf, float(w))
    return jnp.where(lane < group, mean(s2, 2),
                     jnp.where(lane < 2 * group, mean(s4, 4),
                               jnp.where(lane < 3 * group, mean(s8, 8), mean(s16, 16))))


def _rope_tables(pos):
    half = HEAD_DIM // 2
    inv = jnp.power(ROPE_THETA, -jnp.arange(half, dtype=F32) / half)
    ang = pos.astype(F32)[:, None] * inv[None, :]
    cos, sin = jnp.cos(ang), jnp.sin(ang)
    reps = LANES // HEAD_DIM
    return (jnp.tile(jnp.concatenate([cos, cos], axis=1), (1, reps)),
            jnp.tile(jnp.concatenate([-sin, sin], axis=1), (1, reps)))


def _proj_kernel(x_ref, nm_ref, win_ref, cos_ref, sin_ref, wpool_ref, pscale_ref, convw_ref,
                 *refs, wa, wb, wc):
    (k_ref, v_ref, qt_ref, ka_ref, vt_ref, pc_ref, km_ref, ptail_ref, ctail_ref,
     prevu_ref, prevz_ref) = refs[-11:]
    layer = k_ref.shape[0] - 1
    if layer:
        kprev_ref, vprev_ref = refs[:2]
        k_ref[:layer] = kprev_ref[...]
        v_ref[:layer] = vprev_ref[...]
    i = pl.program_id(1)
    tm = x_ref.shape[1]
    n_chunks = wa // LANES

    @pl.when(i == 0)
    def _():
        prevu_ref[...] = jnp.zeros_like(prevu_ref)
        prevz_ref[...] = jnp.zeros_like(prevz_ref)

    h = _rms(x_ref[0], nm_ref[...]).astype(BF16)
    cos, sin = cos_ref[...], sin_ref[...]
    lane = lax.broadcasted_iota(jnp.int32, (tm, LANES), 1)
    low = lane < HEAD_DIM
    block_onehot = (lane - HEAD_DIM == i).astype(F32)
    row_t = lax.broadcasted_iota(jnp.int32, (HEAD_DIM, tm), 0)
    zeros_t = jnp.zeros((HEAD_DIM, tm), BF16)
    ones_row_t = (row_t == 0).astype(BF16)
    scale = HEAD_DIM ** -0.5

    wide = {}
    for c in range(n_chunks):
        sl = slice(c * LANES, (c + 1) * LANES)
        if c % (MXU_DIM // LANES) == 0:
            cols = slice(c * LANES, c * LANES + MXU_DIM)
            wide = {name: _dot(h, win_ref[:, off + cols.start:off + cols.stop])
                    for name, off in (("q", 0), ("k", wa), ("v", 2 * wa))}
        part = slice((c * LANES) % MXU_DIM, (c * LANES) % MXU_DIM + LANES)
        q = _rope_chunk(wide["q"][:, part], cos, sin) * scale
        k = _rope_chunk(wide["k"][:, part], cos, sin)
        v = wide["v"][:, part]
        km_ref[0, 0, :, sl] = jnp.mean(k, axis=0, keepdims=True)
        q_t, k_t, v_t = q.T, k.T, v.T
        for hh in range(2):
            head = 2 * c + hh
            rows = slice(hh * HEAD_DIM, (hh + 1) * HEAD_DIM)
            k_ref[layer, 0, head] = k_t[rows]
            v_ref[layer, 0, head] = v_t[rows]
            qt_ref[0, head] = jnp.concatenate([q_t[rows].astype(BF16), zeros_t], axis=0)
            vt_ref[0, head, 0] = jnp.concatenate([v_t[rows].astype(BF16), ones_row_t], axis=0)
            kh = k if hh == 0 else pltpu.roll(k, HEAD_DIM, 1)
            ka_ref[0, head, 0] = jnp.where(low, kh, block_onehot).astype(BF16)

    o = 3 * wa
    u = _dot(h, win_ref[:, o:o + wb])
    hc = _dot(h, win_ref[:, o + wb:o + wb + wc])
    bg = _dot(h, win_ref[:, o + wb + wc:o + wb + 2 * wc])
    cg = _dot(h, win_ref[:, o + wb + 2 * wc:o + wb + 3 * wc])

    pos = i * tm + lax.broadcasted_iota(jnp.int32, (tm, 1), 0)
    e = jnp.concatenate([prevu_ref[...], u], axis=0)
    s2 = e + pltpu.roll(e, 1, 0)
    s4 = s2 + pltpu.roll(s2, 2, 0)
    s8 = s4 + pltpu.roll(s4, 4, 0)
    s16 = s8 + pltpu.roll(s8, 8, 0)
    t = slice(POOL_CARRY, POOL_CARRY + tm)
    d = _pool_select(s2[t], s4[t], s8[t], s16[t], pos) - u
    p = _dot(d.astype(BF16), wpool_ref[...]) * pscale_ref[...]
    prevu_ref[...] = u[tm - POOL_CARRY:]
    ptail_ref[0] = u[tm - POOL_CARRY:]

    z = cg * hc
    ze = jnp.concatenate([prevz_ref[...], z], axis=0)
    tz = slice(CONV_CARRY, CONV_CARRY + tm)
    conv = (z * convw_ref[2:3, :] + pltpu.roll(ze, 1, 0)[tz] * convw_ref[1:2, :]
            + pltpu.roll(ze, 2, 0)[tz] * convw_ref[0:1, :])
    prevz_ref[...] = z[tm - CONV_CARRY:]
    ctail_ref[0] = z[tm - CONV_CARRY:]
    pc_ref[:, :wb] = p.astype(BF16)
    pc_ref[:, wb:] = (bg * conv).astype(BF16)


def _proj_prompt(x, nm, w_in, cos, sin, wpool_bd, pscale, conv_w, *, n_heads, layer, depth,
                 kv_all=None):
    b, s, d = x.shape
    tm = MOBA_BLOCK
    nb = s // tm
    wa = n_heads * HEAD_DIM
    wb = wpool_bd.shape[1]
    wc = conv_w.shape[2]
    assert s % tm == 0 and nb <= LANES - HEAD_DIM and wa % MXU_DIM == 0
    assert w_in.shape[2] == 3 * wa + wb + 3 * wc
    seq = lambda w: pl.BlockSpec((1, tm, w), lambda bi, i: (bi, i, 0))
    tab = pl.BlockSpec((tm, LANES), lambda bi, i: (i, 0))
    per_b = lambda r, w: pl.BlockSpec((1, r, w), lambda bi, i: (bi, 0, 0))
    stack_t = lambda n: pl.BlockSpec((n, 1, n_heads, HEAD_DIM, tm), lambda bi, i: (0, bi, 0, 0, i))
    head_t = stack_t(layer + 1)
    out_shape = (
        jax.ShapeDtypeStruct((layer + 1, b, n_heads, HEAD_DIM, s), F32),
        jax.ShapeDtypeStruct((layer + 1, b, n_heads, HEAD_DIM, s), F32),
        jax.ShapeDtypeStruct((b, n_heads, LANES, s), BF16),
        jax.ShapeDtypeStruct((b, n_heads, nb, tm, LANES), BF16),
        jax.ShapeDtypeStruct((b, n_heads, nb, LANES, tm), BF16),
        jax.ShapeDtypeStruct((b * s, wb + wc), BF16),
        jax.ShapeDtypeStruct((b, nb, 1, wa), F32),
        jax.ShapeDtypeStruct((b, POOL_CARRY, wb), F32),
        jax.ShapeDtypeStruct((b, CONV_CARRY, wc), F32),
    )
    out_specs = (
        head_t, head_t,
        pl.BlockSpec((1, n_heads, LANES, tm), lambda bi, i: (bi, 0, 0, i)),
        pl.BlockSpec((1, n_heads, 1, tm, LANES), lambda bi, i: (bi, 0, i, 0, 0)),
        pl.BlockSpec((1, n_heads, 1, LANES, tm), lambda bi, i: (bi, 0, i, 0, 0)),
        pl.BlockSpec((tm, wb + wc), lambda bi, i: (bi * nb + i, 0)),
        pl.BlockSpec((1, 1, 1, wa), lambda bi, i: (bi, i, 0, 0)),
        per_b(POOL_CARRY, wb), per_b(CONV_CARRY, wc),
    )
    args = [x, nm, w_in, cos, sin, wpool_bd, pscale, conv_w]
    res = lambda w: _layer_resident(w, layer)
    in_specs = [seq(d), res(nm), res(w_in), tab, tab, res(wpool_bd), res(pscale), res(conv_w)]
    assert (kv_all is None) == (layer == 0)
    if kv_all is not None:
        args += list(kv_all)
        in_specs += [stack_t(layer)] * 2
    return pl.pallas_call(
        functools.partial(_proj_kernel, wa=wa, wb=wb, wc=wc),
        grid=(b, nb),
        in_specs=in_specs,
        out_specs=out_specs,
        out_shape=out_shape,
        scratch_shapes=[pltpu.VMEM((POOL_CARRY, wb), F32), pltpu.VMEM((CONV_CARRY, wc), F32)],
        compiler_params=_params(("parallel", "arbitrary")),
        name="proj_prompt",
    )(*args)


def _top_blocks_t(gate_t, n_valid):
    blk = lax.broadcasted_iota(jnp.int32, gate_t.shape, 0)
    blk_f = blk.astype(F32)
    g = jnp.where(blk < n_valid, gate_t, -jnp.inf)
    picked = jnp.zeros(gate_t.shape, jnp.bool_)
    for _ in range(MOBA_TOPK):
        m = jnp.max(g, axis=0, keepdims=True)
        first = jnp.min(jnp.where(g == m, blk_f, float(2 * LANES)), axis=0, keepdims=True)
        hit = (blk_f == first) & (m > -jnp.inf)
        picked = picked | hit
        g = jnp.where(hit, -jnp.inf, g)
    return jnp.where(picked, 0.0, NEG)


def _logits(a, b):
    return jnp.dot(a, b, preferred_element_type=F32).astype(BF16)


ATTN_BLOCKS_PER_TRIP = 2
ATTN_HEADS_PER_STEP = 8


def _decode_gates_step(page_refs, dq_ref, gate_ref, gacc_ref, lin, *, pages_per_block, n_pages,
                       n_groups, n_grid_steps):
    n_step = len(page_refs)
    n_heads = dq_ref.shape[1]

    @pl.when(lin == 0)
    def _():
        gacc_ref[...] = jnp.zeros_like(gacc_ref)

    def accumulate():
        first_page = lax.rem(lin * n_step, n_pages)
        first_block = lax.div(first_page, pages_per_block)
        lane = lax.broadcasted_iota(jnp.int32, (n_heads, LANES), 1)
        q = dq_ref[0]
        gate = jnp.where(first_page == 0, 0.0, gacc_ref[...])
        for r in range(n_step // pages_per_block):
            tot = page_refs[r * pages_per_block][0, 0]
            for t in range(1, pages_per_block):
                tot = tot + page_refs[r * pages_per_block + t][0, 0]
            k_sum = jnp.sum(tot, axis=2, keepdims=True)
            val = jnp.sum(k_sum * q, axis=1) / float(MOBA_BLOCK)
            gate = jnp.where(lane == first_block + r, val, gate)
        gacc_ref[...] = gate
        gate_ref[0] = gate

    if n_groups == n_grid_steps:
        accumulate()
    else:
        pl.when(lin < n_groups)(accumulate)


def _attn_kernel(pt_ref, q_ref, ka_ref, vt_ref, kmr_ref, *refs, decode):
    n_step = decode["n_step"]
    page_refs, dq_ref = refs[:n_step], refs[n_step]
    o_ref, gate_ref, qs_ref, m_ref, acc_ref, s_ref, gacc_ref = refs[n_step + 1:]
    lin = (pl.program_id(0) * pl.num_programs(1) + pl.program_id(1)) * pl.num_programs(2) \
        + pl.program_id(2)
    _decode_gates_step(page_refs, dq_ref, gate_ref, gacc_ref, lin, pages_per_block=decode["ppb"],
                       n_pages=decode["n_pages"], n_groups=decode["n_groups"],
                       n_grid_steps=decode["n_grid_steps"])

    i = pl.program_id(2)
    tq = q_ref.shape[3]
    nb = ka_ref.shape[2]
    nbp = -(-nb // 8) * 8
    key = lax.broadcasted_iota(jnp.int32, (MOBA_BLOCK, tq), 0)
    qry = lax.broadcasted_iota(jnp.int32, (MOBA_BLOCK, tq), 1)
    n_h = q_ref.shape[1]
    g = ATTN_BLOCKS_PER_TRIP
    last_group = nb // g - 1

    def store_logits(slot, jj, hh):
        keys = ka_ref[0, hh, pl.ds(g * jj, g)].reshape(g * MOBA_BLOCK, LANES)
        s_ref[slot, hh] = _logits(keys, qs_ref[hh])

    heads = range(n_h)
    gates, diag, probs = [], [], []
    for hh in heads:
        km = kmr_ref[0, hh, HEAD_DIM:HEAD_DIM + nbp]
        km_hi = km.astype(BF16)
        km_lo = (km - km_hi.astype(F32)).astype(BF16)
        gates.append(_dot(km_hi, q_ref[0, hh]) + _dot(km_lo, q_ref[0, hh]))
    for hh in heads:
        diag.append(jnp.where(key <= qry, _logits(ka_ref[0, hh, i], q_ref[0, hh]), NEG))
    for hh in heads:
        pen_t = _top_blocks_t(gates[hh], i)
        pen_t = jnp.concatenate([jnp.zeros((HEAD_DIM, tq), F32), pen_t,
                                 jnp.zeros((LANES - HEAD_DIM - nbp, tq), F32)], axis=0)
        qs_ref[hh] = (q_ref[0, hh].astype(F32) + pen_t).astype(BF16)
    for hh in heads:
        m = jnp.max(diag[hh], axis=0, keepdims=True)
        m_ref[hh] = m.astype(F32)
        probs.append(jnp.exp(diag[hh] - m))
    for hh in heads:
        acc_ref[hh] = _dot(vt_ref[0, hh, i], probs[hh])
    for hh in heads:
        store_logits(0, 0, hh)

    n_trips = (i + g - 1) // g

    def trip(jj, slot, prefetch):
        nxt = jnp.minimum(jj + 1, last_group)
        for hh in range(n_h):
            s = s_ref[slot, hh]
            m = m_ref[hh]
            m_new = jnp.maximum(m, jnp.max(s, axis=0, keepdims=True).astype(F32))
            p = jnp.exp(s - m_new.astype(BF16))
            v = jnp.concatenate([vt_ref[0, hh, g * jj + t] for t in range(g)], axis=1)
            acc_ref[hh] = acc_ref[hh] * jnp.exp(m - m_new) + _dot(v, p)
            m_ref[hh] = m_new
            if prefetch:
                store_logits(1 - slot, nxt, hh)

    @pl.loop(0, n_trips // 2)
    def _(tt):
        trip(2 * tt, 0, True)
        trip(2 * tt + 1, 1, True)

    @pl.when(n_trips % 2 == 1)
    def _():
        trip(n_trips - 1, 0, False)

    for p in range(n_h // 2):
        pair = [acc_ref[hh][:HEAD_DIM] / acc_ref[hh][HEAD_DIM:HEAD_DIM + 1] for hh in (2 * p, 2 * p + 1)]
        o_ref[:, p * LANES:(p + 1) * LANES] = jnp.concatenate(pair, axis=0).T.astype(BF16)


def _attn_prompt(q, ka, vt, kmt, cache_kt, page_table, dq4, *, layer):
    b, n_heads, _, s = q.shape
    nb = ka.shape[2]
    tq = MOBA_BLOCK
    hg = ATTN_HEADS_PER_STEP
    assert nb % ATTN_BLOCKS_PER_TRIP == 0 and n_heads % hg == 0 and hg % 2 == 0
    n_hq = n_heads // hg
    n_steps = b * n_hq * nb

    page_size = cache_kt.shape[-1]
    db, n_pages = page_table.shape
    ppb = MOBA_BLOCK // page_size
    assert MOBA_BLOCK % page_size == 0 and n_pages % ppb == 0 and n_pages // ppb <= LANES
    n_step = next(p for p in range(ppb, n_pages + 1, ppb)
                  if n_pages % p == 0 and db * (n_pages // p) <= n_steps)
    n_groups = db * (n_pages // n_step)

    def group(bi, hp, i):
        return jnp.minimum((bi * n_hq + hp) * nb + i, n_groups - 1)

    def page_spec(r):
        def index(bi, hp, i, pt):
            return (layer, pt[group(bi, hp, i) * n_step + r], 0, 0, 0)
        return pl.BlockSpec((1, 1, n_heads, HEAD_DIM, page_size), index)

    seq_of = lambda bi, hp, i: lax.div(group(bi, hp, i), n_pages // n_step)
    decode = dict(n_step=n_step, ppb=ppb, n_pages=n_pages, n_groups=n_groups, n_grid_steps=n_steps)
    return pl.pallas_call(
        functools.partial(_attn_kernel, decode=decode),
        grid_spec=pltpu.PrefetchScalarGridSpec(
            num_scalar_prefetch=1,
            grid=(b, n_hq, nb),
            in_specs=[
                pl.BlockSpec((1, hg, LANES, tq), lambda bi, hp, i, pt: (bi, hp, 0, i)),
                pl.BlockSpec((1, hg, nb, tq, LANES), lambda bi, hp, i, pt: (bi, hp, 0, 0, 0),
                             pipeline_mode=pl.Buffered(1)),
                pl.BlockSpec((1, hg, nb, LANES, tq), lambda bi, hp, i, pt: (bi, hp, 0, 0, 0),
                             pipeline_mode=pl.Buffered(1)),
                pl.BlockSpec((1, hg, LANES, LANES), lambda bi, hp, i, pt: (bi, hp, 0, 0),
                             pipeline_mode=pl.Buffered(1)),
            ] + [page_spec(r) for r in range(n_step)] + [
                pl.BlockSpec((1, n_heads, HEAD_DIM, 1),
                             lambda bi, hp, i, pt: (seq_of(bi, hp, i), 0, 0, 0)),
            ],
            out_specs=[
                pl.BlockSpec((tq, hg * HEAD_DIM), lambda bi, hp, i, pt: (bi * nb + i, hp)),
                pl.BlockSpec((1, n_heads, LANES), lambda bi, hp, i, pt: (seq_of(bi, hp, i), 0, 0)),
            ],
            scratch_shapes=[
                pltpu.VMEM((hg, LANES, tq), BF16),
                pltpu.VMEM((hg, 1, tq), F32),
                pltpu.VMEM((hg, LANES, tq), F32),
                pltpu.VMEM((2, hg, ATTN_BLOCKS_PER_TRIP * MOBA_BLOCK, tq), BF16),
                pltpu.VMEM((n_heads, LANES), F32),
            ],
        ),
        out_shape=[jax.ShapeDtypeStruct((b * s, n_heads * HEAD_DIM), BF16),
                   jax.ShapeDtypeStruct((db, n_heads, LANES), F32)],
        compiler_params=pltpu.CompilerParams(
            dimension_semantics=("arbitrary", "arbitrary", "arbitrary"),
            vmem_limit_bytes=ATTN_VMEM_LIMIT),
        name="attn_prompt",
    )(page_table.reshape(-1), q, ka, vt, kmt, *([cache_kt] * n_step), dq4)


def _gate_table(kmean, n_heads):
    b, nb = kmean.shape[:2]
    t = kmean.reshape(b, nb, n_heads, HEAD_DIM).transpose(0, 2, 1, 3)
    return jnp.pad(t, ((0, 0), (0, 0), (HEAD_DIM, LANES - HEAD_DIM - nb), (0, LANES - HEAD_DIM)))


def _sproj_kernel(x_ref, nm_ref, win_ref, cos_ref, sin_ref, wpool_ref, pscale_ref, convw_ref,
                  spool_ref, sconv_ref, q_ref, k_ref, v_ref, pc_ref, u_ref, z_ref,
                  *, wa, wb, wc, pos):
    h = _rms(x_ref[...], nm_ref[...]).astype(BF16)
    cos, sin = cos_ref[...], sin_ref[...]
    scale = HEAD_DIM ** -0.5
    for c in range(wa // LANES):
        sl = slice(c * LANES, (c + 1) * LANES)
        q_ref[:, sl] = _rope_chunk(_dot(h, win_ref[:, sl]), cos, sin) * scale
        k_ref[:, sl] = _rope_chunk(_dot(h, win_ref[:, wa + c * LANES: wa + (c + 1) * LANES]), cos, sin)
        v_ref[:, sl] = _dot(h, win_ref[:, 2 * wa + c * LANES: 2 * wa + (c + 1) * LANES])
    o = 3 * wa
    u = _dot(h, win_ref[:, o:o + wb])
    hc = _dot(h, win_ref[:, o + wb:o + wb + wc])
    bg = _dot(h, win_ref[:, o + wb + wc:o + wb + 2 * wc])
    cg = _dot(h, win_ref[:, o + wb + 2 * wc:o + wb + 3 * wc])

    sums, run = {}, u
    for r in range(1, max(POOL_WINDOWS)):
        run = run + spool_ref[:, POOL_STATE - r, :]
        if r + 1 in POOL_WINDOWS:
            sums[r + 1] = run
    posv = jnp.full((u.shape[0], 1), pos, jnp.int32)
    d = _pool_select(sums[2], sums[4], sums[8], sums[16], posv) - u
    p = _dot(d.astype(BF16), wpool_ref[...]) * pscale_ref[...]

    z = cg * hc
    conv = (z * convw_ref[2:3, :] + sconv_ref[:, 1, :] * convw_ref[1:2, :]
            + sconv_ref[:, 0, :] * convw_ref[0:1, :])
    u_ref[...] = u
    z_ref[...] = z
    pc_ref[:, :wb] = p.astype(BF16)
    pc_ref[:, wb:] = (bg * conv).astype(BF16)


def _proj_sample(x, nm, w_in, cos, sin, wpool_bd, pscale, conv_w, spool, sconv, *, n_heads, pos):
    m, _ = x.shape
    wa = n_heads * HEAD_DIM
    wb = wpool_bd.shape[0]
    wc = conv_w.shape[1]
    assert spool.shape[1] == POOL_STATE and sconv.shape[1] == CONV_K - 1
    f = lambda w, dt: jax.ShapeDtypeStruct((m, w), dt)
    return pl.pallas_call(
        functools.partial(_sproj_kernel, wa=wa, wb=wb, wc=wc, pos=pos),
        out_shape=(f(wa, F32), f(wa, F32), f(wa, F32), f(wb + wc, BF16), f(wb, F32), f(wc, F32)),
        compiler_params=pltpu.CompilerParams(vmem_limit_bytes=VMEM_LIMIT),
        name="proj_sample",
    )(x, nm, w_in, cos, sin, wpool_bd, pscale, conv_w, spool, sconv)


def _stopk_kernel(gate_ref, idx_ref, *, n_blocks):
    gate = gate_ref[...]
    lane = lax.broadcasted_iota(jnp.int32, gate.shape, 1)
    lane_f = lane.astype(F32)
    gt = jnp.where(lane < n_blocks, gate, -jnp.inf)
    picks = jnp.zeros(gate.shape, F32)
    for r in range(MOBA_TOPK):
        mx = jnp.max(gt, axis=1, keepdims=True)
        first = jnp.min(jnp.where(gt == mx, lane_f, float(LANES)), axis=1, keepdims=True)
        gt = jnp.where(lane_f == first, -jnp.inf, gt)
        picks = jnp.where(lane == r, first, picks)
    idx_ref[...] = picks.astype(jnp.int32)


def _sample_topk(gates, n_blocks):
    assert MOBA_TOPK <= n_blocks <= LANES
    flat = gates.reshape(-1, LANES)
    return pl.pallas_call(
        functools.partial(_stopk_kernel, n_blocks=n_blocks),
        out_shape=jax.ShapeDtypeStruct(flat.shape, jnp.int32),
        name="sample_topk",
    )(flat).reshape(gates.shape)


def _sattn_kernel(idx_ref, pt_ref, *refs, n_fetch):
    k_refs = refs[:2 * n_fetch]
    v_refs = refs[2 * n_fetch:4 * n_fetch]
    q_ref, kn_ref, vn_ref, o_ref = refs[4 * n_fetch:]
    for hh in range(2):
        q = q_ref[0, hh]
        s_self = jnp.sum(q * kn_ref[0, hh], axis=0, keepdims=True)
        scores = [jnp.sum(k_refs[hh * n_fetch + t][0, 0, 0] * q, axis=0, keepdims=True)
                  for t in range(n_fetch)]
        m = s_self
        for s in scores:
            m = jnp.maximum(m, jnp.max(s, axis=1, keepdims=True))
        p_self = jnp.exp(s_self - m)
        l = p_self
        acc = p_self * vn_ref[0, hh]
        for t, s in enumerate(scores):
            p = jnp.exp(s - m)
            l = l + jnp.sum(p, axis=1, keepdims=True)
            acc = acc + jnp.sum(v_refs[hh * n_fetch + t][0, 0, 0] * p, axis=1, keepdims=True)
        o_ref[0, hh] = acc / l


def _sample_attn(cache_kt, cache_vt, page_table, idx, q4, kn4, vn4, *, layer):
    _, n_pool, n_heads, _, page_size = cache_kt.shape
    b = page_table.shape[0]
    ppb = MOBA_BLOCK // page_size
    n_fetch = MOBA_TOPK * ppb

    def page_spec(hh, r, t):
        def index(bi, hp, idx_s, pt):
            head = 2 * hp + hh
            blk = idx_s[(bi * n_heads + head) * MOBA_TOPK + r]
            return (layer, pt[bi, blk * ppb + t], head, 0, 0)
        return pl.BlockSpec((1, 1, 1, HEAD_DIM, page_size), index)

    fetch = [page_spec(hh, r, t) for hh in range(2) for r in range(MOBA_TOPK) for t in range(ppb)]
    vec = pl.BlockSpec((1, 2, HEAD_DIM, 1), lambda bi, hp, idx_s, pt: (bi, hp, 0, 0))
    return pl.pallas_call(
        functools.partial(_sattn_kernel, n_fetch=n_fetch),
        grid_spec=pltpu.PrefetchScalarGridSpec(
            num_scalar_prefetch=2,
            grid=(b, n_heads // 2),
            in_specs=fetch + fetch + [vec, vec, vec],
            out_specs=vec,
        ),
        out_shape=jax.ShapeDtypeStruct((b, n_heads, HEAD_DIM, 1), F32),
        compiler_params=_params(("parallel", "arbitrary")),
        name="sample_attn",
    )(idx.reshape(-1), page_table, *([cache_kt] * len(fetch)), *([cache_vt] * len(fetch)),
      q4, kn4, vn4)


def kernel(x_prompt, x_sample, cache_k, cache_v, page_table, state_pool, state_conv, norm_ffn1, ffn1_gate, ffn1_up, ffn1_down, norm_mix, w_in, w_pool, pool_scale, conv_w, w_o, norm_ffn2, ffn2_gate, ffn2_up, ffn2_down, norm_final):
    b, s, d = x_prompt.shape
    db, t_new, _ = x_sample.shape
    depth, n_pool, page_size, n_heads, head_dim = cache_k.shape
    assert head_dim == HEAD_DIM and t_new == 1 and n_heads % 2 == 0
    past_len = page_table.shape[1] * page_size
    assert past_len % MOBA_BLOCK == 0 and past_len + 1 >= max(POOL_WINDOWS)
    wa = n_heads * HEAD_DIM
    tm_prompt = 512 if (b * s) % 512 == 0 else MOBA_BLOCK

    cos_p, sin_p = _rope_tables(jnp.arange(s, dtype=jnp.int32))
    cos_s, sin_s = _rope_tables(past_len + jnp.arange(1, dtype=jnp.int32))
    row = lambda a: a.reshape(1, -1)
    bf = lambda a: a.astype(BF16)

    cache_kt = cache_k.transpose(0, 1, 3, 4, 2)
    cache_vt = cache_v.transpose(0, 1, 3, 4, 2)

    xp = x_prompt.reshape(b * s, d)
    xs = x_sample.reshape(db, d)
    outs = {n: [] for n in ("ks", "vs", "pp", "ps", "cp", "cs")}
    kv_all = None
    rows = lambda a: a.reshape(depth, 1, -1)
    n_groups = w_pool.shape[1]
    wpool_bd = bf(jnp.stack([jax.scipy.linalg.block_diag(*[w_pool[l, g] for g in range(n_groups)])
                             for l in range(depth)]))
    f1 = (rows(norm_ffn1), bf(ffn1_gate), bf(ffn1_up), bf(ffn1_down))
    f2 = (rows(norm_ffn2), bf(ffn2_gate), bf(ffn2_up), bf(ffn2_down))
    mixer = (rows(norm_mix), bf(w_in), wpool_bd, rows(pool_scale), conv_w)
    wo = bf(w_o)
    for l in range(depth):
        last = l == depth - 1
        final = row(norm_final) if last else None
        nm_l, w_in_l, wpool_l, pscale_l, convw_l = (w[l] for w in mixer)

        xp = _ffn(xp, *f1, layer=l, tm=tm_prompt)
        kp, vp, q, ka, vt, pc, kmean, ptail, ctail = _proj_prompt(
            xp.reshape(b, s, d), mixer[0], mixer[1], cos_p, sin_p, *mixer[2:], n_heads=n_heads,
            layer=l, depth=depth, kv_all=kv_all)
        kv_all = (kp, vp)
        xs = _ffn(xs, *f1, layer=l, tm=db)
        qs, ksn, vsn, pcs, us, zs = _proj_sample(
            xs, nm_l, w_in_l, cos_s, sin_s, wpool_l, pscale_l, convw_l, state_pool[l],
            state_conv[l], n_heads=n_heads, pos=past_len)
        col = lambda a: a.reshape(db, n_heads, HEAD_DIM, 1)

        a, gates = _attn_prompt(q, ka, vt, _gate_table(kmean, n_heads), cache_kt, page_table,
                                col(qs), layer=l)
        xp = _ffn(xp, *f2, layer=l, mix=(a, pc, wo), final=final, tm=tm_prompt)
        outs["pp"].append(ptail[:, POOL_CARRY - POOL_STATE:])
        outs["cp"].append(ctail[:, CONV_CARRY - (CONV_K - 1):])

        idx = _sample_topk(gates, past_len // MOBA_BLOCK)
        a_s = _sample_attn(cache_kt, cache_vt, page_table, idx[:, :, :MOBA_TOPK],
                           col(qs), col(ksn), col(vsn), layer=l)
        xs = _ffn(xs, *f2, layer=l, mix=(bf(a_s.reshape(db, wa)), pcs, wo), final=final, tm=db)
        outs["ks"].append(ksn.reshape(db, 1, n_heads, HEAD_DIM))
        outs["vs"].append(vsn.reshape(db, 1, n_heads, HEAD_DIM))
        outs["ps"].append(jnp.concatenate([state_pool[l][:, 1:], us[:, None]], axis=1))
        outs["cs"].append(jnp.concatenate([state_conv[l][:, 1:], zs[:, None]], axis=1))

    st = lambda n: jnp.stack(outs[n])
    k_prompt, v_prompt = (a.transpose(0, 1, 4, 2, 3) for a in kv_all)
    return (xp.reshape(b, s, d), xs.reshape(db, 1, d), k_prompt, v_prompt, st("ks"), st("vs"),
            st("pp"), st("ps"), st("cp"), st("cs"))
```
